```python
import math
import jax, jax.numpy as jnp
from jax import lax
import numpy as np

D_MODEL = 1024
BATCH = 4
SEQ = 4096
DEPTH = 4
DEC_BATCH = 128
DEC_SEQ = 4
PAST_LEN = 2048
PAGE_SIZE = 128

N_A = DEPTH // 2
N_B = DEPTH - N_A
N_DENSE = (DEPTH + 1) // 2
N_MOE = DEPTH // 2
POOL_WINDOWS = (2, 4, 8, 16)
N_POOL_GROUPS = len(POOL_WINDOWS)
POOL_GROUP = D_MODEL // N_POOL_GROUPS
POOL_BUF = max(POOL_WINDOWS) - 1
N_HEADS = 8
HEAD_DIM = D_MODEL // (2 * N_HEADS)
V_DIM = 2 * HEAD_DIM
QK_WIDTH = N_HEADS * 2 * HEAD_DIM
V_WIDTH = N_HEADS * V_DIM
D_FF = 2816
N_EXPERTS = 8
TOP_K = 2
D_FF_EXPERT = 3584
PLE_DIM = 256
Q_BLOCK = 128
ALPHA = (2 * DEPTH) ** 0.25
BETA = (8 * DEPTH) ** -0.25
LN_EPS = 1e-5
RMS_EPS = 1e-5
NEG_INF = -1e30

kernel_name = 'yoco_pool_diffattn_decoder_step'


def layer_norm(x, g, b):
    xf = x.astype(jnp.float32)
    mu = jnp.mean(xf, -1, keepdims=True)
    var = jnp.mean(jnp.square(xf - mu), -1, keepdims=True)
    return ((xf - mu) * lax.rsqrt(var + LN_EPS) * g + b).astype(x.dtype)


def swiglu(x, w_gu, w_down):
    gate, up = jnp.split(x @ w_gu, 2, axis=-1)
    return (jax.nn.silu(gate) * up) @ w_down


def moe(x, w_router, w_gu, w_down):
    logits = (x @ w_router).astype(jnp.float32)
    top_val, top_idx = lax.top_k(logits, TOP_K)
    top_w = jax.nn.softmax(top_val, axis=-1)
    gates = jnp.sum(jax.nn.one_hot(top_idx, N_EXPERTS, dtype=jnp.float32) * top_w[..., None], axis=-2)
    y = jnp.zeros_like(x)
    for e in range(N_EXPERTS):
        y = y + gates[..., e:e + 1].astype(x.dtype) * swiglu(x, w_gu[e], w_down[e])
    return y


def pool_mixer(x, buf, pos0, w, scale):
    B, T, D = x.shape
    xc = jnp.concatenate([buf.astype(x.dtype), x], axis=1)
    xf = xc.astype(jnp.float32)
    cs = jnp.concatenate([jnp.zeros((B, 1, D), jnp.float32), jnp.cumsum(xf, axis=1)], axis=1)
    end = cs[:, POOL_BUF + 1:]
    pos = pos0 + jnp.arange(T)
    means = []
    for g, w_len in enumerate(POOL_WINDOWS):
        lo, hi = g * POOL_GROUP, (g + 1) * POOL_GROUP
        start = cs[:, POOL_BUF + 1 - w_len:POOL_BUF + 1 - w_len + T, lo:hi]
        cnt = jnp.minimum(w_len, pos + 1).astype(jnp.float32)[None, :, None]
        means.append((end[..., lo:hi] - start) / cnt)
    d = (jnp.concatenate(means, axis=-1) - xf[:, POOL_BUF:]).astype(x.dtype)
    y = jnp.einsum('btgc,gce->btge', d.reshape(B, T, N_POOL_GROUPS, POOL_GROUP), w).reshape(B, T, D) * scale
    return y, xc[:, -POOL_BUF:]


def alibi_slopes():
    return 2.0 ** (-8.0 * jnp.arange(1, N_HEADS + 1, dtype=jnp.float32) / N_HEADS)


def diff_attend(q, q_pos, segments, lam, slopes):
    scores = []
    for k, _, k_pos in segments:
        s = jnp.einsum('bqhmd,bkhmd->bmhqk', q, k).astype(jnp.float32) * (HEAD_DIM ** -0.5)
        dist = (q_pos[:, None] - k_pos[None, :]).astype(jnp.float32)
        s = jnp.where(dist >= 0, s - slopes[:, None, None] * dist, NEG_INF)
        scores.append(s)
    probs = jax.nn.softmax(jnp.concatenate(scores, axis=-1), axis=-1)
    a = probs[:, 0] - lam * probs[:, 1]
    out = None
    off = 0
    for k, v, _ in segments:
        n = k.shape[1]
        o = jnp.einsum('bhqk,bkhe->bqhe', a[..., off:off + n].astype(v.dtype), v)
        out = o if out is None else out + o
        off += n
    return out


def diff_mixer(x, q_pos, segments, w_q, lam_qk, subln_g, w_o, layer_idx, blocked):
    B, T, _ = x.shape
    q = (x @ w_q).reshape(B, T, N_HEADS, 2, HEAD_DIM)
    lam_init = 0.8 - 0.6 * math.exp(-0.3 * layer_idx)
    lq = lam_qk.astype(jnp.float32)
    lam = jnp.exp(jnp.sum(lq[0] * lq[1])) - jnp.exp(jnp.sum(lq[2] * lq[3])) + lam_init
    slopes = alibi_slopes()
    if blocked:
        nb = T // Q_BLOCK
        qb = jnp.moveaxis(q.reshape(B, nb, Q_BLOCK, N_HEADS, 2, HEAD_DIM), 1, 0)
        pb = q_pos.reshape(nb, Q_BLOCK)
        o = lax.map(lambda a: diff_attend(a[0], a[1], segments, lam, slopes), (qb, pb))
        o = jnp.moveaxis(o, 0, 1).reshape(B, T, N_HEADS, V_DIM)
    else:
        o = diff_attend(q, q_pos, segments, lam, slopes)
    of = o.astype(jnp.float32)
    of = of * lax.rsqrt(jnp.mean(of * of, -1, keepdims=True) + RMS_EPS)
    o = (of * subln_g.reshape(N_HEADS, V_DIM) * (1.0 - lam_init)).astype(x.dtype)
    return o.reshape(B, T, V_WIDTH) @ w_o


def shared_kv(x, g, b, w_kv):
    B, T, _ = x.shape
    h = layer_norm(x, g, b) @ w_kv
    k = h[..., :QK_WIDTH].reshape(B, T, N_HEADS, 2 * HEAD_DIM)
    v = h[..., QK_WIDTH:].reshape(B, T, N_HEADS, V_DIM)
    return k, v


def run_trunk(x, p, pool_bufs, past_k, past_v, W, blocked):
    B, T, _ = x.shape
    pos0 = 0 if past_k is None else past_k.shape[1]
    q_pos = pos0 + jnp.arange(T, dtype=jnp.int32)
    new_bufs = []
    segments = None
    k_new = v_new = None
    for i in range(DEPTH):
        if i < N_A:
            mix, nb = pool_mixer(x, pool_bufs[i], pos0, W['pool_w'][i], W['pool_scale'][i])
            new_bufs.append(nb)
        else:
            if segments is None:
                k_new, v_new = shared_kv(x, W['ln_kv_g'], W['ln_kv_b'], W['w_kv'])
                segments = [(k_new.reshape(B, T, N_HEADS, 2, HEAD_DIM), v_new, q_pos)]
                if past_k is not None:
                    segments = [(past_k.reshape(B, pos0, N_HEADS, 2, HEAD_DIM), past_v,
                                 jnp.arange(pos0, dtype=jnp.int32))] + segments
            j = i - N_A
            mix = diff_mixer(x, q_pos, segments, W['w_q'][j], W['lam_qk'][j], W['subln_g'][j],
                             W['w_o'][j], i, blocked)
        x = layer_norm(ALPHA * x + mix, W['ln1_g'][i], W['ln1_b'][i])
        if i % 2 == 0:
            ff = swiglu(x, W['dense_gu'][i // 2], W['dense_down'][i // 2])
        else:
            ff = moe(x, W['w_router'][i // 2], W['moe_gu'][i // 2], W['moe_down'][i // 2])
        x = layer_norm(ALPHA * x + ff, W['ln2_g'][i], W['ln2_b'][i])
        x = x + (p[i] @ W['ple_proj'][i]) * jax.nn.sigmoid(x @ W['ple_gate'][i])
    return x, jnp.stack(new_bufs), k_new, v_new


def setup_inputs(seed: int = 0) -> dict:
    key = jax.random.key(seed)
    ks = iter(jax.random.split(key, 40))
    nrm = lambda shape, scale: jax.random.normal(next(ks), shape, jnp.float32) * scale
    n_pages = PAST_LEN // PAGE_SIZE
    n_used = DEC_BATCH * n_pages
    n_pool = n_used + max(1, n_used // 4)
    page_table = jax.random.permutation(next(ks), n_pool)[:n_used].reshape(DEC_BATCH, n_pages).astype(jnp.int32)
    w_kv_scale = jnp.concatenate([jnp.ones((QK_WIDTH,), jnp.float32), jnp.full((V_WIDTH,), BETA, jnp.float32)])
    return {
        'x_prompt': nrm((BATCH, SEQ, D_MODEL), 1.0),
        'x_sample': nrm((DEC_BATCH, DEC_SEQ, D_MODEL), 1.0),
        'state_pool': nrm((N_A, DEC_BATCH, POOL_BUF, D_MODEL), 1.0),
        'cache_k': nrm((n_pool, PAGE_SIZE, N_HEADS, 2 * HEAD_DIM), 1.0),
        'cache_v': nrm((n_pool, PAGE_SIZE, N_HEADS, V_DIM), BETA),
        'page_table': page_table,
        'p_prompt': nrm((DEPTH, BATCH, SEQ, PLE_DIM), 1.0),
        'p_sample': nrm((DEPTH, DEC_BATCH, DEC_SEQ, PLE_DIM), 1.0),
        'pool_w': nrm((N_A, N_POOL_GROUPS, POOL_GROUP, POOL_GROUP), POOL_GROUP ** -0.5 * BETA),
        'pool_scale': 1.0 + nrm((N_A, D_MODEL), 0.1),
        'ln1_g': 1.0 + nrm((DEPTH, D_MODEL), 0.02),
        'ln1_b': nrm((DEPTH, D_MODEL), 0.02),
        'ln2_g': 1.0 + nrm((DEPTH, D_MODEL), 0.02),
        'ln2_b': nrm((DEPTH, D_MODEL), 0.02),
        'ln_kv_g': 1.0 + nrm((D_MODEL,), 0.02),
        'ln_kv_b': nrm((D_MODEL,), 0.02),
        'w_kv': nrm((D_MODEL, QK_WIDTH + V_WIDTH), D_MODEL ** -0.5) * w_kv_scale,
        'w_q': nrm((N_B, D_MODEL, QK_WIDTH), D_MODEL ** -0.5),
        'lam_qk': nrm((N_B, 4, HEAD_DIM), 0.1),
        'subln_g': 1.0 + nrm((N_B, V_WIDTH), 0.02),
        'w_o': nrm((N_B, V_WIDTH, D_MODEL), V_WIDTH ** -0.5 * BETA),
        'dense_gu': nrm((N_DENSE, D_MODEL, 2 * D_FF), D_MODEL ** -0.5),
        'dense_down': nrm((N_DENSE, D_FF, D_MODEL), D_FF ** -0.5 * BETA),
        'w_router': nrm((N_MOE, D_MODEL, N_EXPERTS), D_MODEL ** -0.5),
        'moe_gu': nrm((N_MOE, N_EXPERTS, D_MODEL, 2 * D_FF_EXPERT), D_MODEL ** -0.5),
        'moe_down': nrm((N_MOE, N_EXPERTS, D_FF_EXPERT, D_MODEL), D_FF_EXPERT ** -0.5 * BETA),
        'ple_proj': nrm((DEPTH, PLE_DIM, D_MODEL), PLE_DIM ** -0.5),
        'ple_gate': nrm((DEPTH, D_MODEL, D_MODEL), D_MODEL ** -0.5),
    }


def reference(x_prompt, x_sample, state_pool, cache_k, cache_v, page_table, p_prompt, p_sample,
              pool_w, pool_scale, ln1_g, ln1_b, ln2_g, ln2_b, ln_kv_g, ln_kv_b, w_kv, w_q, lam_qk,
              subln_g, w_o, dense_gu, dense_down, w_router, moe_gu, moe_down, ple_proj, ple_gate):
    W = {'pool_w': pool_w, 'pool_scale': pool_scale, 'ln1_g': ln1_g, 'ln1_b': ln1_b,
         'ln2_g': ln2_g, 'ln2_b': ln2_b, 'ln_kv_g': ln_kv_g, 'ln_kv_b': ln_kv_b, 'w_kv': w_kv,
         'w_q': w_q, 'lam_qk': lam_qk, 'subln_g': subln_g, 'w_o': w_o, 'dense_gu': dense_gu,
         'dense_down': dense_down, 'w_router': w_router, 'moe_gu': moe_gu, 'moe_down': moe_down,
         'ple_proj': ple_proj, 'ple_gate': ple_gate}
    zero_bufs = jnp.zeros((N_A, x_prompt.shape[0], POOL_BUF, D_MODEL), x_prompt.dtype)
    y_prompt, pool_prompt, k_prompt, v_prompt = run_trunk(x_prompt, p_prompt, zero_bufs, None, None, W, True)
    n_seq, n_pages = page_table.shape
    past_k = cache_k[page_table].reshape(n_seq, n_pages * PAGE_SIZE, N_HEADS, 2 * HEAD_DIM)
    past_v = cache_v[page_table].reshape(n_seq, n_pages * PAGE_SIZE, N_HEADS, V_DIM)
    y_sample, pool_sample, k_sample, v_sample = run_trunk(x_sample, p_sample, state_pool, past_k, past_v, W, False)
    return (y_prompt, y_sample, pool_prompt, pool_sample, k_prompt, v_prompt, k_sample, v_sample)
```

```python
import functools
import math

import numpy as np
import jax
import jax.numpy as jnp
from jax import lax
from jax.experimental import pallas as pl
from jax.experimental.pallas import tpu as pltpu

F32 = jnp.float32
BF16 = jnp.bfloat16

POOL_WINDOWS = (2, 4, 8, 16)
POOL_BUF = max(POOL_WINDOWS) - 1
POOL_HALO = 16
TOP_K = 2
LN_EPS = 1e-5
RMS_EPS = 1e-5
NEG_INF = -1e30
LANES = 128
VMEM_LIMIT = 56 * 1024 * 1024


def _cparams(sem):
    return pltpu.CompilerParams(dimension_semantics=sem, vmem_limit_bytes=VMEM_LIMIT)


def _layer_norm(x, g, b):
    mu = jnp.mean(x, axis=-1, keepdims=True)
    xc = x - mu
    var = jnp.mean(xc * xc, axis=-1, keepdims=True)
    return xc * lax.rsqrt(var + LN_EPS) * g + b


def _sigmoid(x):
    return 1.0 / (1.0 + jnp.exp(-x))


def _dot(a, b):
    return jnp.dot(a, b, preferred_element_type=F32)


def _dot_nt(a, b):
    return lax.dot_general(a, b, (((1,), (1,)), ((), ())), preferred_element_type=F32)


def _pick_tile(n, target):
    t = min(n, target)
    while n % t:
        t //= 2
    return t


def _pool_prompt_kernel(x_ref, halo_ref, w_ref, scale_ref, g_ref, b_ref, o_ref, ext_ref, *, tm, alpha):
    i = pl.program_id(1)
    x = x_ref[0]
    ext_ref[0:POOL_HALO, :] = jnp.where(i > 0, halo_ref[0], 0.0)
    ext_ref[POOL_HALO:, :] = x
    gw = x.shape[-1] // len(POOL_WINDOWS)
    pos = lax.broadcasted_iota(jnp.int32, (tm, 1), 0) + i * tm
    ys = []
    for g, w in enumerate(POOL_WINDOWS):
        lo, hi = g * gw, (g + 1) * gw
        xs = x[:, lo:hi]
        s = xs
        for k in range(1, w):
            s = s + ext_ref[POOL_HALO - k:POOL_HALO - k + tm, lo:hi]
        cnt = jnp.minimum(w, pos + 1).astype(F32)
        d = s / cnt - xs
        ys.append(_dot(d.astype(BF16), w_ref[g]))
    y = jnp.concatenate(ys, axis=-1) * scale_ref[...]
    o_ref[...] = _layer_norm(alpha * x + y, g_ref[...], b_ref[...])


def _pool_prompt(x, w_bf, scale, g, b, n_all, alpha):
    B, T, D = x.shape
    tm = _pick_tile(T, 512)
    nt = T // tm
    hb = tm // POOL_HALO
    G, gw = w_bf.shape[0], w_bf.shape[1]
    return pl.pallas_call(
        functools.partial(_pool_prompt_kernel, tm=tm, alpha=alpha),
        out_shape=jax.ShapeDtypeStruct((n_all, D), F32),
        grid=(B, nt),
        in_specs=[
            pl.BlockSpec((1, tm, D), lambda bb, i: (bb, i, 0)),
            pl.BlockSpec((1, POOL_HALO, D), lambda bb, i: (bb, jnp.maximum(i * hb - 1, 0), 0)),
            pl.BlockSpec((G, gw, gw), lambda bb, i: (0, 0, 0)),
            pl.BlockSpec((1, D), lambda bb, i: (0, 0)),
            pl.BlockSpec((1, D), lambda bb, i: (0, 0)),
            pl.BlockSpec((1, D), lambda bb, i: (0, 0)),
        ],
        out_specs=pl.BlockSpec((tm, D), lambda bb, i: (bb * nt + i, 0)),
        scratch_shapes=[pltpu.VMEM((tm + POOL_HALO, D), F32)],
        compiler_params=_cparams(("parallel", "arbitrary")),
        name="pool_prompt",
    )(x, x, w_bf, scale, g, b)


def _pool_sample_kernel(buf_ref, x_ref, w_ref, scale_ref, g_ref, b_ref, o_ref, *, alpha, pos0):
    nb, ts = buf_ref.shape[0], x_ref.shape[0]
    rows = [buf_ref[r] for r in range(nb)] + [x_ref[t] for t in range(ts)]
    gw = rows[0].shape[-1] // len(POOL_WINDOWS)
    for t in range(ts):
        x = rows[nb + t]
        ys = []
        for g, w in enumerate(POOL_WINDOWS):
            lo, hi = g * gw, (g + 1) * gw
            xs = x[:, lo:hi]
            s = xs
            for k in range(1, w):
                s = s + rows[nb + t - k][:, lo:hi]
            d = s / float(min(w, pos0 + t + 1)) - xs
            ys.append(_dot(d.astype(BF16), w_ref[g]))
        y = jnp.concatenate(ys, axis=-1) * scale_ref[...]
        o_ref[t] = _layer_norm(alpha * x + y, g_ref[...], b_ref[...])


def _pool_sample(buf_t, x_t, w_bf, scale, g, b, alpha, pos0):
    nb, Bs, D = buf_t.shape
    ts = x_t.shape[0]
    bs = _pick_tile(Bs, 32)
    G, gw = w_bf.shape[0], w_bf.shape[1]
    return pl.pallas_call(
        functools.partial(_pool_sample_kernel, alpha=alpha, pos0=pos0),
        out_shape=jax.ShapeDtypeStruct((ts, Bs, D), F32),
        grid=(Bs // bs,),
        in_specs=[
            pl.BlockSpec((nb, bs, D), lambda i: (0, i, 0)),
            pl.BlockSpec((ts, bs, D), lambda i: (0, i, 0)),
            pl.BlockSpec((G, gw, gw), lambda i: (0, 0, 0)),
            pl.BlockSpec((1, D), lambda i: (0, 0)),
            pl.BlockSpec((1, D), lambda i: (0, 0)),
            pl.BlockSpec((1, D), lambda i: (0, 0)),
        ],
        out_specs=pl.BlockSpec((ts, bs, D), lambda i: (0, i, 0)),
        compiler_params=_cparams(("parallel",)),
        name="pool_sample",
    )(buf_t, x_t, w_bf, scale, g, b)


def _ffn_kernel(x_ref, gates_ref, wg_ref, wu_ref, wd_ref, g_ref, b_ref, p_ref, proj_ref, gw_ref,
                o_ref, acc_ref, xb_ref, *, alpha, use_gates):
    e, j = pl.program_id(1), pl.program_id(2)

    @pl.when((e == 0) & (j == 0))
    def _():
        acc_ref[...] = jnp.zeros_like(acc_ref)
        xb_ref[...] = x_ref[...].astype(BF16)

    xb = xb_ref[...]
    h = _dot(xb, wg_ref[0])
    u = _dot(xb, wu_ref[0])
    a = h * _sigmoid(h) * u
    if use_gates:
        lane = lax.broadcasted_iota(jnp.int32, gates_ref.shape, 1)
        a = a * jnp.sum(jnp.where(lane == e, gates_ref[...], 0.0), axis=-1, keepdims=True)
    acc_ref[...] += _dot(a.astype(BF16), wd_ref[0])

    @pl.when((e == pl.num_programs(1) - 1) & (j == pl.num_programs(2) - 1))
    def _():
        x2 = _layer_norm(alpha * x_ref[...] + acc_ref[...], g_ref[...], b_ref[...])
        pp = _dot(p_ref[...].astype(BF16), proj_ref[...])
        gt = _dot(x2.astype(BF16), gw_ref[...])
        o_ref[...] = x2 + pp * _sigmoid(gt)


def _ffn(x, gates, w_gu, w_down, g, b, p, proj, gate_w, alpha, use_gates):
    N, D = x.shape
    E, F = w_down.shape[0], w_down.shape[1]
    P = p.shape[1]
    tm = _pick_tile(N, 512)
    tf = 512 if F % 512 == 0 else 256
    tf = min(tf, F)
    nf = F // tf
    return pl.pallas_call(
        functools.partial(_ffn_kernel, alpha=alpha, use_gates=use_gates),
        out_shape=jax.ShapeDtypeStruct((N, D), F32),
        grid=(N // tm, E, nf),
        in_specs=[
            pl.BlockSpec((tm, D), lambda i, e, j: (i, 0)),
            pl.BlockSpec((tm, LANES), lambda i, e, j: (i, 0)),
            pl.BlockSpec((1, D, tf), lambda i, e, j: (e, 0, j)),
            pl.BlockSpec((1, D, tf), lambda i, e, j: (e, 0, nf + j)),
            pl.BlockSpec((1, tf, D), lambda i, e, j: (e, j, 0)),
            pl.BlockSpec((1, D), lambda i, e, j: (0, 0)),
            pl.BlockSpec((1, D), lambda i, e, j: (0, 0)),
            pl.BlockSpec((tm, P), lambda i, e, j: (i, 0)),
            pl.BlockSpec((P, D), lambda i, e, j: (0, 0)),
            pl.BlockSpec((D, D), lambda i, e, j: (0, 0)),
        ],
        out_specs=pl.BlockSpec((tm, D), lambda i, e, j: (i, 0)),
        scratch_shapes=[pltpu.VMEM((tm, D), F32), pltpu.VMEM((tm, D), BF16)],
        compiler_params=_cparams(("parallel", "arbitrary", "arbitrary")),
        name="ffn_moe" if use_gates else "ffn_dense",
    )(x, gates, w_gu, w_gu, w_down, g, b, p, proj, gate_w)


def _router_kernel(x_ref, wr_ref, o_ref, *, n_e):
    lg = jnp.dot(x_ref[...], wr_ref[...], preferred_element_type=F32, precision=lax.Precision.HIGHEST)
    lane = lax.broadcasted_iota(jnp.int32, lg.shape, 1)
    lg = jnp.where(lane < n_e, lg, -jnp.inf)
    v1 = jnp.max(lg, axis=-1, keepdims=True)
    i1 = jnp.min(jnp.where(lg == v1, lane, LANES), axis=-1, keepdims=True)
    lg2 = jnp.where(lane == i1, -jnp.inf, lg)
    v2 = jnp.max(lg2, axis=-1, keepdims=True)
    i2 = jnp.min(jnp.where(lg2 == v2, lane, LANES), axis=-1, keepdims=True)
    e2 = jnp.exp(v2 - v1)
    den = 1.0 + e2
    o_ref[...] = jnp.where(lane == i1, 1.0 / den, 0.0) + jnp.where(lane == i2, e2 / den, 0.0)


def _router(x, wr_pad, n_e):
    N, D = x.shape
    tm = _pick_tile(N, 512)
    return pl.pallas_call(
        functools.partial(_router_kernel, n_e=n_e),
        out_shape=jax.ShapeDtypeStruct((N, LANES), F32),
        grid=(N // tm,),
        in_specs=[pl.BlockSpec((tm, D), lambda i: (i, 0)),
                  pl.BlockSpec((D, LANES), lambda i: (0, 0))],
        out_specs=pl.BlockSpec((tm, LANES), lambda i: (i, 0)),
        compiler_params=_cparams(("parallel",)),
        name="router",
    )(x, wr_pad)


def _kv_kernel(x_ref, g_ref, b_ref, w_ref, k_ref, v_ref, kb_ref, vb_ref):
    xn = _layer_norm(x_ref[...], g_ref[...], b_ref[...]).astype(BF16)
    qk = k_ref.shape[-1]
    k = _dot(xn, w_ref[:, :qk])
    v = _dot(xn, w_ref[:, qk:])
    k_ref[...] = k
    v_ref[...] = v
    kb_ref[...] = k.astype(BF16)
    vb_ref[...] = v.astype(BF16)


def _shared_kv(x, g, b, w_kv_bf, qk_width):
    N, D = x.shape
    W = w_kv_bf.shape[1]
    vw = W - qk_width
    tm = _pick_tile(N, 512)
    row = lambda i: (i, 0)
    return pl.pallas_call(
        _kv_kernel,
        out_shape=(jax.ShapeDtypeStruct((N, qk_width), F32), jax.ShapeDtypeStruct((N, vw), F32),
                   jax.ShapeDtypeStruct((N, qk_width), BF16), jax.ShapeDtypeStruct((N, vw), BF16)),
        grid=(N // tm,),
        in_specs=[pl.BlockSpec((tm, D), row),
                  pl.BlockSpec((1, D), lambda i: (0, 0)),
                  pl.BlockSpec((1, D), lambda i: (0, 0)),
                  pl.BlockSpec((D, W), lambda i: (0, 0))],
        out_specs=(pl.BlockSpec((tm, qk_width), row), pl.BlockSpec((tm, vw), row),
                   pl.BlockSpec((tm, qk_width), row), pl.BlockSpec((tm, vw), row)),
        compiler_params=_cparams(("parallel",)),
        name="shared_kv",
    )(x, g, b, w_kv_bf)


def _proj_kernel(x_ref, w_ref, o_ref, *, scale):
    o_ref[...] = (_dot(x_ref[...].astype(BF16), w_ref[...]) * scale).astype(o_ref.dtype)


def _proj(x, w_bf, scale):
    N, D = x.shape
    W = w_bf.shape[1]
    tm = _pick_tile(N, 512)
    return pl.pallas_call(
        functools.partial(_proj_kernel, scale=scale),
        out_shape=jax.ShapeDtypeStruct((N, W), BF16),
        grid=(N // tm,),
        in_specs=[pl.BlockSpec((tm, D), lambda i: (i, 0)),
                  pl.BlockSpec((D, W), lambda i: (0, 0))],
        out_specs=pl.BlockSpec((tm, W), lambda i: (i, 0)),
        compiler_params=_cparams(("parallel",)),
        name="q_proj",
    )(x, w_bf)


def _out_ln_kernel(o_ref, w_ref, x_ref, g_ref, b_ref, y_ref, *, alpha):
    mix = _dot(o_ref[...], w_ref[...])
    y_ref[...] = _layer_norm(alpha * x_ref[...] + mix, g_ref[...], b_ref[...])


def _out_ln(o_bf, w_bf, x, g, b, alpha):
    N, D = x.shape
    W = o_bf.shape[1]
    tm = _pick_tile(N, 512)
    return pl.pallas_call(
        functools.partial(_out_ln_kernel, alpha=alpha),
        out_shape=jax.ShapeDtypeStruct((N, D), F32),
        grid=(N // tm,),
        in_specs=[pl.BlockSpec((tm, W), lambda i: (i, 0)),
                  pl.BlockSpec((W, D), lambda i: (0, 0)),
                  pl.BlockSpec((tm, D), lambda i: (i, 0)),
                  pl.BlockSpec((1, D), lambda i: (0, 0)),
                  pl.BlockSpec((1, D), lambda i: (0, 0))],
        out_specs=pl.BlockSpec((tm, D), lambda i: (i, 0)),
        compiler_params=_cparams(("parallel",)),
        name="attn_out_ln",
    )(o_bf, w_bf, x, g, b)


def _lambda_value(lam_ref, lam_init):
    lq = lam_ref[...]
    s1 = jnp.sum(lq[0:1] * lq[1:2], axis=-1, keepdims=True)
    s2 = jnp.sum(lq[2:3] * lq[3:4], axis=-1, keepdims=True)
    return jnp.exp(s1) - jnp.exp(s2) + lam_init


def _head_norm(o, g, lam_init):
    return o * lax.rsqrt(jnp.mean(o * o, axis=-1, keepdims=True) + RMS_EPS) * g * (1.0 - lam_init)


def _softmax_update(s, v, m_ref, l_ref, acc_ref, idx):
    m_old = m_ref[idx]
    m_new = jnp.maximum(m_old, jnp.max(s, axis=-1, keepdims=True))
    a = jnp.exp(m_old - m_new)
    p = jnp.exp(s - m_new)
    l_ref[idx] = a * l_ref[idx] + jnp.sum(p, axis=-1, keepdims=True)
    acc_ref[idx] = a * acc_ref[idx] + _dot(p.astype(BF16), v)
    m_ref[idx] = m_new


def _attn_prompt_kernel(slopes_ref, lam_ref, q_ref, k_ref, v_ref, g_ref, o_ref, m_ref, l_ref, acc_ref,
                        *, tq, hd, lam_init):
    h, i, j = pl.program_id(1), pl.program_id(2), pl.program_id(3)

    @pl.when(j == 0)
    def _():
        m_ref[...] = jnp.full_like(m_ref, NEG_INF)
        l_ref[...] = jnp.zeros_like(l_ref)
        acc_ref[...] = jnp.zeros_like(acc_ref)

    @pl.when(j <= i)
    def _():
        q, k, v = q_ref[...], k_ref[...], v_ref[...]
        lane = lax.broadcasted_iota(jnp.int32, q.shape, 1)
        r = lax.broadcasted_iota(jnp.int32, (tq, tq), 0)
        c = lax.broadcasted_iota(jnp.int32, (tq, tq), 1)
        dist = (i - j) * tq + r - c
        bias = slopes_ref[h] * dist.astype(F32)
        for m in range(2):
            qm = jnp.where((lane >= m * hd) & (lane < (m + 1) * hd), q, jnp.zeros_like(q))
            s = jnp.where(dist >= 0, _dot_nt(qm, k) - bias, NEG_INF)
            _softmax_update(s, v, m_ref, l_ref, acc_ref, m)

    @pl.when(j == i)
    def _():
        lam = _lambda_value(lam_ref, lam_init)
        o = acc_ref[0] / l_ref[0] - lam * (acc_ref[1] / l_ref[1])
        o_ref[...] = _head_norm(o, g_ref[...], lam_init).astype(o_ref.dtype)


def _attn_prompt(q, k, v, slopes, lam_qk, subln_g, B, T, H, lam_init):
    n_all, W = q.shape
    vd = W // H
    hd = vd // 2
    tq = _pick_tile(T, 512)
    nq = T // tq
    return pl.pallas_call(
        functools.partial(_attn_prompt_kernel, tq=tq, hd=hd, lam_init=lam_init),
        out_shape=jax.ShapeDtypeStruct((n_all, W), BF16),
        grid=(B, H, nq, nq),
        in_specs=[
            pl.BlockSpec(memory_space=pltpu.SMEM),
            pl.BlockSpec(lam_qk.shape, lambda b, h, i, j: (0, 0)),
            pl.BlockSpec((tq, vd), lambda b, h, i, j: (b * nq + i, h)),
            pl.BlockSpec((tq, vd), lambda b, h, i, j: (b * nq + jnp.minimum(j, i), h)),
            pl.BlockSpec((tq, vd), lambda b, h, i, j: (b * nq + jnp.minimum(j, i), h)),
            pl.BlockSpec((1, vd), lambda b, h, i, j: (0, h)),
        ],
        out_specs=pl.BlockSpec((tq, vd), lambda b, h, i, j: (b * nq + i, h)),
        scratch_shapes=[pltpu.VMEM((2, tq, 1), F32), pltpu.VMEM((2, tq, 1), F32),
                        pltpu.VMEM((2, tq, vd), F32)],
        compiler_params=_cparams(("parallel", "parallel", "parallel", "arbitrary")),
        name="attn_prompt",
    )(slopes, lam_qk, q, k, v, subln_g)


def _attn_sample_kernel(pt_ref, lam_ref, qz_ref, k_ref, v_ref, kn_ref, vn_ref, bias_ref, cvec_ref,
                        biasn_ref, g_ref, o_ref, m_ref, l_ref, acc_ref, *, n_pages, lam_init):
    p = pl.program_id(1)

    @pl.when(p == 0)
    def _():
        m_ref[...] = jnp.full_like(m_ref, NEG_INF)
        l_ref[...] = jnp.zeros_like(l_ref)
        acc_ref[...] = jnp.zeros_like(acc_ref)

    @pl.when(p < n_pages)
    def _():
        kb = k_ref[0].astype(BF16)
        s = _dot_nt(qz_ref[0], kb) + bias_ref[...] + p.astype(F32) * cvec_ref[...]
        _softmax_update(s, v_ref[0].astype(BF16), m_ref, l_ref, acc_ref, 0)

    @pl.when(p == n_pages)
    def _():
        s = _dot_nt(qz_ref[0], kn_ref[0]) + biasn_ref[...]
        _softmax_update(s, vn_ref[0], m_ref, l_ref, acc_ref, 0)
        lam = _lambda_value(lam_ref, lam_init)
        o = acc_ref[0] / l_ref[0]
        half = o.shape[0] // 2
        o = o[:half] - lam * o[half:]
        o_ref[0] = _head_norm(o, g_ref[...], lam_init).astype(o_ref.dtype)


def _attn_sample(page_table, qz, cache_k2, cache_v2, k_new, v_new, bias, cvec, bias_new, g_rows, lam_qk,
                 lam_init):
    Bs, R, vd = qz.shape
    n_pages = page_table.shape[1]
    rows = cache_k2.shape[1]
    rn = k_new.shape[1]
    pt = page_table.reshape(-1)

    def page_map(b, p, pt_ref):
        return (pt_ref[b * n_pages + jnp.minimum(p, n_pages - 1)], 0, 0)

    const2 = lambda b, p, pt_ref: (0, 0)
    seq3 = lambda b, p, pt_ref: (b, 0, 0)
    return pl.pallas_call(
        functools.partial(_attn_sample_kernel, n_pages=n_pages, lam_init=lam_init),
        out_shape=jax.ShapeDtypeStruct((Bs, R // 2, vd), BF16),
        grid_spec=pltpu.PrefetchScalarGridSpec(
            num_scalar_prefetch=1,
            grid=(Bs, n_pages + 1),
            in_specs=[
                pl.BlockSpec(lam_qk.shape, const2),
                pl.BlockSpec((1, R, vd), seq3),
                pl.BlockSpec((1, rows, vd), page_map),
                pl.BlockSpec((1, rows, vd), page_map),
                pl.BlockSpec((1, rn, vd), seq3),
                pl.BlockSpec((1, rn, vd), seq3),
                pl.BlockSpec(bias.shape, const2),
                pl.BlockSpec(cvec.shape, const2),
                pl.BlockSpec(bias_new.shape, const2),
                pl.BlockSpec(g_rows.shape, const2),
            ],
            out_specs=pl.BlockSpec((1, R // 2, vd), seq3),
            scratch_shapes=[pltpu.VMEM((1, R, 1), F32), pltpu.VMEM((1, R, 1), F32),
                            pltpu.VMEM((1, R, vd), F32)],
        ),
        compiler_params=_cparams(("parallel", "arbitrary")),
        name="attn_sample",
    )(pt, lam_qk, qz, cache_k2, cache_v2, k_new, v_new, bias, cvec, bias_new, g_rows)


def _sample_bias_tables(H, Ts, page, pos0):
    slopes = 2.0 ** (-8.0 * np.arange(1, H + 1, dtype=np.float64) / H)
    hq = np.tile(np.repeat(np.arange(H), Ts), 2)
    tq = np.tile(np.arange(Ts), 2 * H)
    key = np.repeat(np.arange(page), H)
    hk = np.tile(np.arange(H), page)
    dist = (pos0 + tq)[:, None] - key[None, :]
    bias = np.where(hq[:, None] == hk[None, :], -slopes[hq][:, None] * dist, NEG_INF)
    cvec = (slopes[hq] * page)[:, None]
    tn = np.repeat(np.arange(Ts), H)
    hn = np.tile(np.arange(H), Ts)
    dn = tq[:, None] - tn[None, :]
    bias_new = np.where((hq[:, None] == hn[None, :]) & (dn >= 0), -slopes[hq][:, None] * dn, NEG_INF)
    return (jnp.asarray(bias, F32), jnp.asarray(cvec, F32), jnp.asarray(bias_new, F32),
            jnp.asarray(slopes, F32))


def kernel(x_prompt, x_sample, state_pool, cache_k, cache_v, page_table, p_prompt, p_sample, pool_w, pool_scale, ln1_g, ln1_b, ln2_g, ln2_b, ln_kv_g, ln_kv_b, w_kv, w_q, lam_qk, subln_g, w_o, dense_gu, dense_down, w_router, moe_gu, moe_down, ple_proj, ple_gate):
    B, T, D = x_prompt.shape
    Bs, Ts, _ = x_sample.shape
    depth = ln1_g.shape[0]
    n_a = pool_w.shape[0]
    n_pool, page, H, vd = cache_v.shape
    hd = vd // 2
    qk_width = H * 2 * hd
    n_e = w_router.shape[-1]
    n_pages = page_table.shape[1]
    pos0 = n_pages * page
    n_p, n_s = B * T, Bs * Ts
    n_all = n_p + n_s
    alpha = (2 * depth) ** 0.25
    row = lambda a: a.reshape(1, -1)

    pool_w_bf = pool_w.astype(BF16)
    dense_gu_bf, dense_down_bf = dense_gu.astype(BF16), dense_down.astype(BF16)
    moe_gu_bf, moe_down_bf = moe_gu.astype(BF16), moe_down.astype(BF16)
    w_kv_bf, w_q_bf, w_o_bf = w_kv.astype(BF16), w_q.astype(BF16), w_o.astype(BF16)
    proj_bf, gate_bf = ple_proj.astype(BF16), ple_gate.astype(BF16)
    wr_pad = jnp.pad(w_router, ((0, 0), (0, 0), (0, LANES - n_e)))
    ones_gate = jnp.ones((n_all, LANES), F32)

    p_all = jnp.concatenate([p_prompt.reshape(depth, n_p, -1), p_sample.reshape(depth, n_s, -1)], axis=1)
    bias, cvec, bias_new, slopes = _sample_bias_tables(H, Ts, page, pos0)

    x_p = x_prompt
    x_s = x_sample
    x_all = None
    pool_p, pool_s = [], []
    kv = None
    for i in range(depth):
        if i < n_a:
            pool_p.append(x_p[:, T - POOL_BUF:])
            pool_s.append(jnp.concatenate([state_pool[i][:, Ts:], x_s], axis=1))
            x1 = _pool_prompt(x_p, pool_w_bf[i], row(pool_scale[i]), row(ln1_g[i]), row(ln1_b[i]), n_all, alpha)
            x1_s = _pool_sample(state_pool[i].transpose(1, 0, 2), x_s.transpose(1, 0, 2), pool_w_bf[i],
                                row(pool_scale[i]), row(ln1_g[i]), row(ln1_b[i]), alpha, pos0)
            x1 = lax.dynamic_update_slice(x1, x1_s.transpose(1, 0, 2).reshape(n_s, D), (n_p, 0))
        else:
            jj = i - n_a
            lam_init = 0.8 - 0.6 * math.exp(-0.3 * i)
            if kv is None:
                kv = _shared_kv(x_all, row(ln_kv_g), row(ln_kv_b), w_kv_bf, qk_width)
            k_f, v_f, k_b, v_b = kv
            q = _proj(x_all, w_q_bf[jj], hd ** -0.5)
            o = _attn_prompt(q, k_b, v_b, slopes, lam_qk[jj], row(subln_g[jj]), B, T, H, lam_init)
            q_s = q[n_p:].reshape(Bs, Ts, H, vd).transpose(0, 2, 1, 3).reshape(Bs, H * Ts, vd)
            lane = jnp.arange(vd)
            qz = jnp.concatenate([jnp.where(lane < hd, q_s, 0), jnp.where(lane >= hd, q_s, 0)], axis=1)
            g_rows = jnp.repeat(subln_g[jj].reshape(H, vd), Ts, axis=0)
            o_s = _attn_sample(page_table, qz, cache_k.reshape(n_pool, page * H, 2 * hd),
                               cache_v.reshape(n_pool, page * H, vd),
                               k_b[n_p:].reshape(Bs, Ts * H, 2 * hd), v_b[n_p:].reshape(Bs, Ts * H, vd),
                               bias, cvec, bias_new, g_rows, lam_qk[jj], lam_init)
            o_s = o_s.reshape(Bs, H, Ts, vd).transpose(0, 2, 1, 3).reshape(n_s, H * vd)
            o = lax.dynamic_update_slice(o, o_s, (n_p, 0))
            x1 = _out_ln(o, w_o_bf[jj], x_all, row(ln1_g[i]), row(ln1_b[i]), alpha)
        if i % 2 == 0:
            x_all = _ffn(x1, ones_gate, dense_gu_bf[i // 2][None], dense_down_bf[i // 2][None],
                         row(ln2_g[i]), row(ln2_b[i]), p_all[i], proj_bf[i], gate_bf[i], alpha, False)
        else:
            gates = _router(x1, wr_pad[i // 2], n_e)
            x_all = _ffn(x1, gates, moe_gu_bf[i // 2], moe_down_bf[i // 2],
                         row(ln2_g[i]), row(ln2_b[i]), p_all[i], proj_bf[i], gate_bf[i], alpha, True)
        x_p = x_all[:n_p].reshape(B, T, D)
        x_s = x_all[n_p:].reshape(Bs, Ts, D)

    k_f, v_f = kv[0], kv[1]
    return (x_p, x_s, jnp.stack(pool_p), jnp.stack(pool_s),
            k_f[:n_p].reshape(B, T, H, 2 * hd), v_f[:n_p].reshape(B, T, H, vd),
            k_f[n_p:].reshape(Bs, Ts, H, 2 * hd), v_f[n_p:].reshape(Bs, Ts, H, vd))
```

```python
import functools
import math

import numpy as np
import jax
import jax.numpy as jnp
from jax import lax
from jax.experimental import pallas as pl
from jax.experimental.pallas import tpu as pltpu

F32 = jnp.float32
BF16 = jnp.bfloat16

POOL_WINDOWS = (2, 4, 8, 16)
POOL_BUF = max(POOL_WINDOWS) - 1
POOL_HALO = 16
TOP_K = 2
LN_EPS = 1e-5
RMS_EPS = 1e-5
NEG_INF = -1e30
LANES = 128
VMEM_LIMIT = 56 * 1024 * 1024


def _cparams(sem):
    return pltpu.CompilerParams(dimension_semantics=sem, vmem_limit_bytes=VMEM_LIMIT)


def _layer_norm(x, g, b):
    mu = jnp.mean(x, axis=-1, keepdims=True)
    xc = x - mu
    var = jnp.mean(xc * xc, axis=-1, keepdims=True)
    return xc * lax.rsqrt(var + LN_EPS) * g + b


def _sigmoid(x):
    return 1.0 / (1.0 + jnp.exp(-x))


def _dot(a, b):
    return jnp.dot(a, b, preferred_element_type=F32)


def _dot_nt(a, b):
    return lax.dot_general(a, b, (((1,), (1,)), ((), ())), preferred_element_type=F32)


def _pick_tile(n, target):
    t = min(n, target)
    while n % t:
        t //= 2
    return t


def _pool_prompt_kernel(x_ref, halo_ref, tail_ref, w_ref, scale_ref, g_ref, b_ref, o_ref, ext_ref,
                        *, tm, nt, n_main, alpha):
    step = pl.program_id(0)

    @pl.when(step < n_main)
    def _():
        i = step % nt
        x = x_ref[0]
        ext_ref[0:POOL_HALO, :] = jnp.where(i > 0, halo_ref[0], 0.0)
        ext_ref[POOL_HALO:, :] = x
        gw = x.shape[-1] // len(POOL_WINDOWS)
        pos = lax.broadcasted_iota(jnp.int32, (tm, 1), 0) + i * tm
        ys = []
        for g, w in enumerate(POOL_WINDOWS):
            lo, hi = g * gw, (g + 1) * gw
            xs = x[:, lo:hi]
            s = xs
            for k in range(1, w):
                s = s + ext_ref[POOL_HALO - k:POOL_HALO - k + tm, lo:hi]
            cnt = jnp.minimum(w, pos + 1).astype(F32)
            d = s / cnt - xs
            ys.append(_dot(d.astype(BF16), w_ref[g]))
        y = jnp.concatenate(ys, axis=-1) * scale_ref[...]
        o_ref[...] = _layer_norm(alpha * x + y, g_ref[...], b_ref[...])

    @pl.when(step >= n_main)
    def _():
        o_ref[...] = tail_ref[...]


def _pool_prompt(x, tail, w_bf, scale, g, b, alpha):
    B, T, D = x.shape
    n_tail = tail.shape[0]
    tm = _pick_tile(math.gcd(T, n_tail), 512)
    nt = T // tm
    n_main = B * nt
    hb = tm // POOL_HALO
    G, gw = w_bf.shape[0], w_bf.shape[1]
    main = lambda s: jnp.minimum(s, n_main - 1)
    const2 = lambda s: (0, 0)
    return pl.pallas_call(
        functools.partial(_pool_prompt_kernel, tm=tm, nt=nt, n_main=n_main, alpha=alpha),
        out_shape=jax.ShapeDtypeStruct((B * T + n_tail, D), F32),
        grid=(n_main + n_tail // tm,),
        in_specs=[
            pl.BlockSpec((1, tm, D), lambda s: (main(s) // nt, main(s) % nt, 0)),
            pl.BlockSpec((1, POOL_HALO, D),
                         lambda s: (main(s) // nt, jnp.maximum((main(s) % nt) * hb - 1, 0), 0)),
            pl.BlockSpec((tm, D), lambda s: (jnp.maximum(s - n_main, 0), 0)),
            pl.BlockSpec((G, gw, gw), lambda s: (0, 0, 0)),
            pl.BlockSpec((1, D), const2),
            pl.BlockSpec((1, D), const2),
            pl.BlockSpec((1, D), const2),
        ],
        out_specs=pl.BlockSpec((tm, D), lambda s: (s, 0)),
        scratch_shapes=[pltpu.VMEM((tm + POOL_HALO, D), F32)],
        compiler_params=_cparams(("arbitrary",)),
        name="pool_prompt",
    )(x, x, tail, w_bf, scale, g, b)


def _pool_sample_kernel(buf_ref, x_ref, w_ref, scale_ref, g_ref, b_ref, o_ref, *, alpha, pos0):
    nb, ts = buf_ref.shape[0], x_ref.shape[0]
    rows = [buf_ref[r] for r in range(nb)] + [x_ref[t] for t in range(ts)]
    gw = rows[0].shape[-1] // len(POOL_WINDOWS)
    for t in range(ts):
        x = rows[nb + t]
        ys = []
        for g, w in enumerate(POOL_WINDOWS):
            lo, hi = g * gw, (g + 1) * gw
            xs = x[:, lo:hi]
            s = xs
            for k in range(1, w):
                s = s + rows[nb + t - k][:, lo:hi]
            d = s / float(min(w, pos0 + t + 1)) - xs
            ys.append(_dot(d.astype(BF16), w_ref[g]))
        y = jnp.concatenate(ys, axis=-1) * scale_ref[...]
        o_ref[t] = _layer_norm(alpha * x + y, g_ref[...], b_ref[...])


def _pool_sample(buf_t, x_t, w_bf, scale, g, b, alpha, pos0):
    nb, Bs, D = buf_t.shape
    ts = x_t.shape[0]
    bs = _pick_tile(Bs, 32)
    G, gw = w_bf.shape[0], w_bf.shape[1]
    return pl.pallas_call(
        functools.partial(_pool_sample_kernel, alpha=alpha, pos0=pos0),
        out_shape=jax.ShapeDtypeStruct((ts, Bs, D), F32),
        grid=(Bs // bs,),
        in_specs=[
            pl.BlockSpec((nb, bs, D), lambda i: (0, i, 0)),
            pl.BlockSpec((ts, bs, D), lambda i: (0, i, 0)),
            pl.BlockSpec((G, gw, gw), lambda i: (0, 0, 0)),
            pl.BlockSpec((1, D), lambda i: (0, 0)),
            pl.BlockSpec((1, D), lambda i: (0, 0)),
            pl.BlockSpec((1, D), lambda i: (0, 0)),
        ],
        out_specs=pl.BlockSpec((ts, bs, D), lambda i: (0, i, 0)),
        compiler_params=_cparams(("parallel",)),
        name="pool_sample",
    )(buf_t, x_t, w_bf, scale, g, b)


def _ln_ple(x, ff, g_ref, b_ref, p_ref, proj_ref, gw_ref, alpha):
    x2 = _layer_norm(alpha * x + ff, g_ref[...], b_ref[...])
    pp = _dot(p_ref[...].astype(BF16), proj_ref[...])
    gt = _dot(x2.astype(BF16), gw_ref[...])
    return x2 + pp * _sigmoid(gt)


def _swiglu_partial(xb, wg, wu, wd):
    h = _dot(xb, wg)
    u = _dot(xb, wu)
    return _dot((h * _sigmoid(h) * u).astype(BF16), wd)


def _ffn_kernel(x_ref, wg_ref, wu_ref, wd_ref, g_ref, b_ref, p_ref, proj_ref, gw_ref, o_ref, acc_ref, xb_ref,
                *, alpha):
    j = pl.program_id(1)

    @pl.when(j == 0)
    def _():
        acc_ref[...] = jnp.zeros_like(acc_ref)
        xb_ref[...] = x_ref[...].astype(BF16)

    acc_ref[...] += _swiglu_partial(xb_ref[...], wg_ref[...], wu_ref[...], wd_ref[...])

    @pl.when(j == pl.num_programs(1) - 1)
    def _():
        o_ref[...] = _ln_ple(x_ref[...], acc_ref[...], g_ref, b_ref, p_ref, proj_ref, gw_ref, alpha)


def _ffn_tile(F, target):
    best = LANES
    for t in range(LANES, min(F, target) + 1, LANES):
        if F % t == 0:
            best = t
    return best


def _ffn(x, w_gu, w_down, g, b, p, proj, gate_w, alpha):
    N, D = x.shape
    F = w_down.shape[0]
    P = p.shape[1]
    tm = _pick_tile(N, 512)
    tf = _ffn_tile(F, 1536)
    nf = F // tf
    return pl.pallas_call(
        functools.partial(_ffn_kernel, alpha=alpha),
        out_shape=jax.ShapeDtypeStruct((N, D), F32),
        grid=(N // tm, nf),
        in_specs=[
            pl.BlockSpec((tm, D), lambda i, j: (i, 0)),
            pl.BlockSpec((D, tf), lambda i, j: (0, j)),
            pl.BlockSpec((D, tf), lambda i, j: (0, nf + j)),
            pl.BlockSpec((tf, D), lambda i, j: (j, 0)),
            pl.BlockSpec((1, D), lambda i, j: (0, 0)),
            pl.BlockSpec((1, D), lambda i, j: (0, 0)),
            pl.BlockSpec((tm, P), lambda i, j: (i, 0)),
            pl.BlockSpec((P, D), lambda i, j: (0, 0)),
            pl.BlockSpec((D, D), lambda i, j: (0, 0)),
        ],
        out_specs=pl.BlockSpec((tm, D), lambda i, j: (i, 0)),
        scratch_shapes=[pltpu.VMEM((tm, D), F32), pltpu.VMEM((tm, D), BF16)],
        compiler_params=_cparams(("parallel", "arbitrary")),
        name="ffn_dense",
    )(x, w_gu, w_gu, w_down, g, b, p, proj, gate_w)


def _router_kernel(x_ref, wr_ref, o_ref, *, n_e):
    lg = jnp.dot(x_ref[...], wr_ref[...], preferred_element_type=F32, precision=lax.Precision.HIGHEST)
    lane = lax.broadcasted_iota(jnp.int32, lg.shape, 1)
    lg = jnp.where(lane < n_e, lg, -jnp.inf)
    v1 = jnp.max(lg, axis=-1, keepdims=True)
    i1 = jnp.min(jnp.where(lg == v1, lane, LANES), axis=-1, keepdims=True)
    lg2 = jnp.where(lane == i1, -jnp.inf, lg)
    v2 = jnp.max(lg2, axis=-1, keepdims=True)
    i2 = jnp.min(jnp.where(lg2 == v2, lane, LANES), axis=-1, keepdims=True)
    e2 = jnp.exp(v2 - v1)
    den = 1.0 + e2
    o_ref[...] = jnp.where(lane == i1, 1.0 / den, 0.0) + jnp.where(lane == i2, e2 / den, 0.0)


def _router(x, wr_pad, n_e):
    N, D = x.shape
    tm = _pick_tile(N, 512)
    return pl.pallas_call(
        functools.partial(_router_kernel, n_e=n_e),
        out_shape=jax.ShapeDtypeStruct((N, LANES), F32),
        grid=(N // tm,),
        in_specs=[pl.BlockSpec((tm, D), lambda i: (i, 0)),
                  pl.BlockSpec((D, LANES), lambda i: (0, 0))],
        out_specs=pl.BlockSpec((tm, LANES), lambda i: (i, 0)),
        compiler_params=_cparams(("parallel",)),
        name="router",
    )(x, wr_pad)


MOE_ROW_TILE = 1024
MOE_SUB_TILE = 512


def _route_plan(gates, n_e, tm):
    N = gates.shape[0]
    n_tiles = -(-TOP_K * N // tm) + n_e
    sel = gates[:, :n_e] > 0.0
    cnt = jnp.sum(sel, axis=0, dtype=jnp.int32)
    padded = (cnt + tm - 1) // tm * tm
    ends = jnp.cumsum(padded)
    rank = jnp.cumsum(sel.astype(jnp.int32), axis=0) - 1
    pos = (ends - padded)[None, :] + rank
    rows = n_tiles * tm
    pos_lo = jnp.minimum(jnp.min(jnp.where(sel, pos, rows), axis=1), rows - 1)
    pos_hi = jnp.max(jnp.where(sel, pos, -1), axis=1)
    pos_hi = jnp.where(pos_hi < 0, pos_lo, pos_hi)
    tile_start = jnp.arange(n_tiles, dtype=jnp.int32) * tm
    tile_e = jnp.minimum(jnp.searchsorted(ends, tile_start, side="right"), n_e - 1).astype(jnp.int32)
    used = jnp.clip(jnp.take(ends - padded + cnt, tile_e) - tile_start, 0, tm)
    used = jnp.where(tile_start < ends[-1], used, 0).astype(jnp.int32)
    last_e = jnp.max(jnp.where(used > 0, tile_e, 0))
    tile_e = jnp.where(tile_start < ends[-1], tile_e, last_e)
    return pos_lo.astype(jnp.int32), pos_hi.astype(jnp.int32), tile_e, used, rows


def _row_copy(src_ref, src_row, dst_ref, dst_row, sem):
    return pltpu.make_async_copy(src_ref.at[pl.ds(src_row, 1)], dst_ref.at[pl.ds(dst_row, 1)], sem)


def _dispatch_kernel(lo_ref, hi_ref, x_ref, xs_in_ref, xs_ref, sem, *, tm):
    del xs_in_ref
    base = pl.program_id(0) * tm

    def start(r, c):
        _row_copy(x_ref, r, xs_ref, lo_ref[base + r], sem).start()
        _row_copy(x_ref, r, xs_ref, hi_ref[base + r], sem).start()
        return c

    def wait(r, c):
        _row_copy(x_ref, 0, xs_ref, 0, sem).wait()
        _row_copy(x_ref, 0, xs_ref, 0, sem).wait()
        return c

    lax.fori_loop(0, tm, start, 0)
    lax.fori_loop(0, tm, wait, 0)


def _dispatch(pos_lo, pos_hi, x, rows):
    N, D = x.shape
    tm = _pick_tile(N, 256)
    return pl.pallas_call(
        functools.partial(_dispatch_kernel, tm=tm),
        out_shape=jax.ShapeDtypeStruct((rows, D), F32),
        grid_spec=pltpu.PrefetchScalarGridSpec(
            num_scalar_prefetch=2,
            grid=(N // tm,),
            in_specs=[pl.BlockSpec((tm, D), lambda i, lo, hi: (i, 0)),
                      pl.BlockSpec(memory_space=pl.ANY)],
            out_specs=pl.BlockSpec(memory_space=pl.ANY),
            scratch_shapes=[pltpu.SemaphoreType.DMA],
        ),
        input_output_aliases={3: 0},
        compiler_params=_cparams(("arbitrary",)),
        name="moe_dispatch",
    )(pos_lo, pos_hi, x, jnp.zeros((rows, D), F32))


def _experts_kernel(te_ref, used_ref, x_ref, wg_ref, wu_ref, wd_ref, o_ref, acc_ref, xb_ref, *, sub):
    r, j = pl.program_id(0), pl.program_id(1)
    used = used_ref[r]
    n_sub = x_ref.shape[0] // sub

    @pl.when(j == 0)
    def _():
        acc_ref[...] = jnp.zeros_like(acc_ref)
        xb_ref[...] = x_ref[...].astype(BF16)

    for s in range(n_sub):
        @pl.when(used > s * sub)
        def _():
            rs = pl.ds(s * sub, sub)
            acc_ref[rs, :] += _swiglu_partial(xb_ref[rs, :], wg_ref[0], wu_ref[0], wd_ref[0])

    @pl.when(j == pl.num_programs(1) - 1)
    def _():
        o_ref[...] = acc_ref[...]


def _experts(tile_e, used, xs, w_gu, w_down, tm):
    rows, D = xs.shape
    E, F = w_down.shape[0], w_down.shape[1]
    tf = _ffn_tile(F, 512)
    nf = F // tf

    def jmap(r, j, used_ref):
        return jnp.where(used_ref[r] > 0, j, nf - 1)

    return pl.pallas_call(
        functools.partial(_experts_kernel, sub=min(MOE_SUB_TILE, tm)),
        out_shape=jax.ShapeDtypeStruct((rows, D), F32),
        grid_spec=pltpu.PrefetchScalarGridSpec(
            num_scalar_prefetch=2,
            grid=(rows // tm, nf),
            in_specs=[
                pl.BlockSpec((tm, D), lambda r, j, te, us: (r, 0)),
                pl.BlockSpec((1, D, tf), lambda r, j, te, us: (te[r], 0, jmap(r, j, us))),
                pl.BlockSpec((1, D, tf), lambda r, j, te, us: (te[r], 0, nf + jmap(r, j, us))),
                pl.BlockSpec((1, tf, D), lambda r, j, te, us: (te[r], jmap(r, j, us), 0)),
            ],
            out_specs=pl.BlockSpec((tm, D), lambda r, j, te, us: (r, 0)),
            scratch_shapes=[pltpu.VMEM((tm, D), F32), pltpu.VMEM((tm, D), BF16)],
        ),
        compiler_params=_cparams(("parallel", "arbitrary")),
        name="moe_experts",
    )(tile_e, used, xs, w_gu, w_gu, w_down)


def _combine_kernel(lo_ref, hi_ref, x_ref, gates_ref, ys_ref, g_ref, b_ref, p_ref, proj_ref, gw_ref, o_ref,
                    ya_ref, yb_ref, sem, *, tm, alpha):
    base = pl.program_id(0) * tm

    def start(r, c):
        _row_copy(ys_ref, lo_ref[base + r], ya_ref, r, sem).start()
        _row_copy(ys_ref, hi_ref[base + r], yb_ref, r, sem).start()
        return c

    def wait(r, c):
        _row_copy(ys_ref, 0, ya_ref, 0, sem).wait()
        _row_copy(ys_ref, 0, yb_ref, 0, sem).wait()
        return c

    lax.fori_loop(0, tm, start, 0)
    gates = gates_ref[...]
    lane = lax.broadcasted_iota(jnp.int32, gates.shape, 1)
    sel = gates > 0.0
    e_lo = jnp.min(jnp.where(sel, lane, LANES), axis=-1, keepdims=True)
    e_hi = jnp.max(jnp.where(sel, lane, -1), axis=-1, keepdims=True)
    g_lo = jnp.sum(jnp.where(lane == e_lo, gates, 0.0), axis=-1, keepdims=True)
    g_hi = jnp.where(e_hi != e_lo, jnp.sum(jnp.where(lane == e_hi, gates, 0.0), axis=-1, keepdims=True), 0.0)
    lax.fori_loop(0, tm, wait, 0)
    ff = g_lo * ya_ref[...] + g_hi * yb_ref[...]
    o_ref[...] = _ln_ple(x_ref[...], ff, g_ref, b_ref, p_ref, proj_ref, gw_ref, alpha)


def _combine(pos_lo, pos_hi, x, gates, ys, g, b, p, proj, gate_w, alpha):
    N, D = x.shape
    P = p.shape[1]
    tm = _pick_tile(N, 256)
    row = lambda i, lo, hi: (i, 0)
    const = lambda i, lo, hi: (0, 0)
    return pl.pallas_call(
        functools.partial(_combine_kernel, tm=tm, alpha=alpha),
        out_shape=jax.ShapeDtypeStruct((N, D), F32),
        grid_spec=pltpu.PrefetchScalarGridSpec(
            num_scalar_prefetch=2,
            grid=(N // tm,),
            in_specs=[pl.BlockSpec((tm, D), row),
                      pl.BlockSpec((tm, LANES), row),
                      pl.BlockSpec(memory_space=pl.ANY),
                      pl.BlockSpec((1, D), const),
                      pl.BlockSpec((1, D), const),
                      pl.BlockSpec((tm, P), row),
                      pl.BlockSpec((P, D), const),
                      pl.BlockSpec((D, D), const)],
            out_specs=pl.BlockSpec((tm, D), row),
            scratch_shapes=[pltpu.VMEM((tm, D), F32), pltpu.VMEM((tm, D), F32), pltpu.SemaphoreType.DMA],
        ),
        compiler_params=_cparams(("arbitrary",)),
        name="moe_combine",
    )(pos_lo, pos_hi, x, gates, ys, g, b, p, proj, gate_w)


def _moe(x, wr_pad, n_e, w_gu, w_down, g, b, p, proj, gate_w, alpha):
    gates = _router(x, wr_pad, n_e)
    tm = MOE_ROW_TILE
    pos_lo, pos_hi, tile_e, used, rows = _route_plan(gates, n_e, tm)
    xs = _dispatch(pos_lo, pos_hi, x, rows)
    ys = _experts(tile_e, used, xs, w_gu, w_down, tm)
    return _combine(pos_lo, pos_hi, x, gates, ys, g, b, p, proj, gate_w, alpha)


def _kv_kernel(x_ref, g_ref, b_ref, w_ref, k_ref, v_ref, kb_ref, vb_ref):
    xn = _layer_norm(x_ref[...], g_ref[...], b_ref[...]).astype(BF16)
    qk = k_ref.shape[-1]
    k = _dot(xn, w_ref[:, :qk])
    v = _dot(xn, w_ref[:, qk:])
    k_ref[...] = k
    v_ref[...] = v
    kb_ref[...] = k.astype(BF16)
    vb_ref[...] = v.astype(BF16)


def _shared_kv(x, g, b, w_kv_bf, qk_width):
    N, D = x.shape
    W = w_kv_bf.shape[1]
    vw = W - qk_width
    tm = _pick_tile(N, 512)
    row = lambda i: (i, 0)
    return pl.pallas_call(
        _kv_kernel,
        out_shape=(jax.ShapeDtypeStruct((N, qk_width), F32), jax.ShapeDtypeStruct((N, vw), F32),
                   jax.ShapeDtypeStruct((N, qk_width), BF16), jax.ShapeDtypeStruct((N, vw), BF16)),
        grid=(N // tm,),
        in_specs=[pl.BlockSpec((tm, D), row),
                  pl.BlockSpec((1, D), lambda i: (0, 0)),
                  pl.BlockSpec((1, D), lambda i: (0, 0)),
                  pl.BlockSpec((D, W), lambda i: (0, 0))],
        out_specs=(pl.BlockSpec((tm, qk_width), row), pl.BlockSpec((tm, vw), row),
                   pl.BlockSpec((tm, qk_width), row), pl.BlockSpec((tm, vw), row)),
        compiler_params=_cparams(("parallel",)),
        name="shared_kv",
    )(x, g, b, w_kv_bf)


def _proj_kernel(x_ref, w_ref, o_ref, *, scale):
    o_ref[...] = (_dot(x_ref[...].astype(BF16), w_ref[...]) * scale).astype(o_ref.dtype)


def _proj(x, w_bf, scale):
    N, D = x.shape
    W = w_bf.shape[1]
    tm = _pick_tile(N, 512)
    return pl.pallas_call(
        functools.partial(_proj_kernel, scale=scale),
        out_shape=jax.ShapeDtypeStruct((N, W), BF16),
        grid=(N // tm,),
        in_specs=[pl.BlockSpec((tm, D), lambda i: (i, 0)),
                  pl.BlockSpec((D, W), lambda i: (0, 0))],
        out_specs=pl.BlockSpec((tm, W), lambda i: (i, 0)),
        compiler_params=_cparams(("parallel",)),
        name="q_proj",
    )(x, w_bf)


def _out_ln_kernel(op_ref, os_ref, w_ref, x_ref, g_ref, b_ref, y_ref, *, n_main, alpha):
    o = jnp.where(pl.program_id(0) < n_main, op_ref[...], os_ref[...])
    y_ref[...] = _layer_norm(alpha * x_ref[...] + _dot(o, w_ref[...]), g_ref[...], b_ref[...])


def _out_ln(o_p, o_s, w_bf, x, g, b, alpha):
    N, D = x.shape
    W = o_p.shape[1]
    tm = _pick_tile(math.gcd(o_p.shape[0], o_s.shape[0]), 512)
    n_main = o_p.shape[0] // tm
    return pl.pallas_call(
        functools.partial(_out_ln_kernel, n_main=n_main, alpha=alpha),
        out_shape=jax.ShapeDtypeStruct((N, D), F32),
        grid=(N // tm,),
        in_specs=[pl.BlockSpec((tm, W), lambda i: (jnp.minimum(i, n_main - 1), 0)),
                  pl.BlockSpec((tm, W), lambda i: (jnp.maximum(i - n_main, 0), 0)),
                  pl.BlockSpec((W, D), lambda i: (0, 0)),
                  pl.BlockSpec((tm, D), lambda i: (i, 0)),
                  pl.BlockSpec((1, D), lambda i: (0, 0)),
                  pl.BlockSpec((1, D), lambda i: (0, 0))],
        out_specs=pl.BlockSpec((tm, D), lambda i: (i, 0)),
        compiler_params=_cparams(("parallel",)),
        name="attn_out_ln",
    )(o_p, o_s, w_bf, x, g, b)


def _lambda_value(lam_ref, lam_init):
    lq = lam_ref[...]
    s1 = jnp.sum(lq[0:1] * lq[1:2], axis=-1, keepdims=True)
    s2 = jnp.sum(lq[2:3] * lq[3:4], axis=-1, keepdims=True)
    return jnp.exp(s1) - jnp.exp(s2) + lam_init


def _head_norm(o, g, lam_init):
    return o * lax.rsqrt(jnp.mean(o * o, axis=-1, keepdims=True) + RMS_EPS) * g * (1.0 - lam_init)


def _softmax_update(s, v, m_ref, l_ref, acc_ref, idx):
    m_old = m_ref[idx]
    m_new = jnp.maximum(m_old, jnp.max(s, axis=-1, keepdims=True))
    a = jnp.exp(m_old - m_new)
    p = jnp.exp(s - m_new)
    l_ref[idx] = a * l_ref[idx] + jnp.sum(p, axis=-1, keepdims=True)
    acc_ref[idx] = a * acc_ref[idx] + _dot(p.astype(BF16), v)
    m_ref[idx] = m_new


KPOS_RADIX = 256
PROMPT_HEADS_PER_STEP = 2


def _attn_prompt_kernel(slopes_ref, lam_ref, q_ref, k_ref, v_ref, g_ref, o_ref, qa_ref, m_ref, l_ref, acc_ref,
                        *, tq, hd, hp, lam_init):
    hg, i, j = pl.program_id(1), pl.program_id(2), pl.program_id(3)
    vd = 2 * hd

    @pl.when(j == 0)
    def _():
        m_ref[...] = jnp.full_like(m_ref, NEG_INF)
        l_ref[...] = jnp.zeros_like(l_ref)
        acc_ref[...] = jnp.zeros_like(acc_ref)
        lane = lax.broadcasted_iota(jnp.int32, (tq, vd), 1)
        for hh in range(hp):
            q = q_ref[:, hh * vd:(hh + 1) * vd]
            slope = slopes_ref[hg * hp + hh]
            aug = jnp.where(lane == 0, slope * KPOS_RADIX, jnp.where(lane == 1, slope, 0.0)).astype(BF16)
            qa_ref[hh, 0:tq, 0:vd] = jnp.where(lane < hd, q, jnp.zeros_like(q))
            qa_ref[hh, tq:, 0:vd] = jnp.where(lane >= hd, q, jnp.zeros_like(q))
            qa_ref[hh, 0:tq, vd:] = aug
            qa_ref[hh, tq:, vd:] = aug

    def block(diagonal):
        kpos = lax.broadcasted_iota(jnp.int32, (tq, vd), 0) + j * tq
        lane = lax.broadcasted_iota(jnp.int32, (tq, vd), 1)
        kaug = jnp.where(lane == 0, kpos // KPOS_RADIX, jnp.where(lane == 1, kpos % KPOS_RADIX, 0))
        kaug = kaug.astype(F32).astype(BF16)
        if diagonal:
            r = lax.broadcasted_iota(jnp.int32, (2 * tq, tq), 0)
            c = lax.broadcasted_iota(jnp.int32, (2 * tq, tq), 1)
            visible = jnp.where(r >= tq, r - tq, r) >= c
        for hh in range(hp):
            ka = jnp.concatenate([k_ref[:, hh * vd:(hh + 1) * vd], kaug], axis=1)
            s = _dot_nt(qa_ref[hh], ka)
            if diagonal:
                s = jnp.where(visible, s, NEG_INF)
            _softmax_update(s, v_ref[:, hh * vd:(hh + 1) * vd], m_ref, l_ref, acc_ref, hh)

    @pl.when(j < i)
    def _():
        block(False)

    @pl.when(j == i)
    def _():
        block(True)
        lam = _lambda_value(lam_ref, lam_init)
        for hh in range(hp):
            o = acc_ref[hh] / l_ref[hh]
            o = o[:tq] - lam * o[tq:]
            o_ref[:, hh * vd:(hh + 1) * vd] = _head_norm(o, g_ref[:, hh * vd:(hh + 1) * vd],
                                                         lam_init).astype(o_ref.dtype)


def _attn_prompt(q, k, v, slopes, lam_qk, subln_g, B, T, H, lam_init, hp):
    W = q.shape[1]
    vd = W // H
    hd = vd // 2
    tq = _pick_tile(T, 512)
    nq = T // tq
    assert T <= KPOS_RADIX * KPOS_RADIX and H % hp == 0
    return pl.pallas_call(
        functools.partial(_attn_prompt_kernel, tq=tq, hd=hd, hp=hp, lam_init=lam_init),
        out_shape=jax.ShapeDtypeStruct((B * T, W), BF16),
        grid=(B, H // hp, nq, nq),
        in_specs=[
            pl.BlockSpec(memory_space=pltpu.SMEM),
            pl.BlockSpec(lam_qk.shape, lambda b, h, i, j: (0, 0)),
            pl.BlockSpec((tq, hp * vd), lambda b, h, i, j: (b * nq + i, h)),
            pl.BlockSpec((tq, hp * vd), lambda b, h, i, j: (b * nq + jnp.minimum(j, i), h)),
            pl.BlockSpec((tq, hp * vd), lambda b, h, i, j: (b * nq + jnp.minimum(j, i), h)),
            pl.BlockSpec((1, hp * vd), lambda b, h, i, j: (0, h)),
        ],
        out_specs=pl.BlockSpec((tq, hp * vd), lambda b, h, i, j: (b * nq + i, h)),
        scratch_shapes=[pltpu.VMEM((hp, 2 * tq, 2 * vd), BF16), pltpu.VMEM((hp, 2 * tq, 1), F32),
                        pltpu.VMEM((hp, 2 * tq, 1), F32), pltpu.VMEM((hp, 2 * tq, vd), F32)],
        compiler_params=_cparams(("parallel", "parallel", "parallel", "arbitrary")),
        name="attn_prompt",
    )(slopes, lam_qk, q, k, v, subln_g)


SAMPLE_PAGES_PER_STEP = 8


def _attn_sample_kernel(pt_ref, lam_ref, qz_ref, *refs, gp, n_steps, lam_init):
    k_refs, v_refs = refs[:gp], refs[gp:2 * gp]
    kn_ref, vn_ref, bias_ref, biasn_ref, g_ref, o_ref, m_ref, l_ref, acc_ref = refs[2 * gp:]
    st = pl.program_id(1)
    rows = k_refs[0].shape[1]

    @pl.when(st == 0)
    def _():
        m_ref[...] = jnp.full_like(m_ref, NEG_INF)
        l_ref[...] = jnp.zeros_like(l_ref)
        acc_ref[...] = jnp.zeros_like(acc_ref)

    qz = qz_ref[0]
    for step in range(n_steps):
        @pl.when(st == step)
        def _():
            ss = []
            for g in range(gp):
                pg = step * gp + g
                ss.append(_dot_nt(qz, k_refs[g][0].astype(BF16)) + bias_ref[:, pg * rows:(pg + 1) * rows])
            vb = jnp.concatenate([v_refs[g][0].astype(BF16) for g in range(gp)], axis=0)
            _softmax_update(jnp.concatenate(ss, axis=1), vb, m_ref, l_ref, acc_ref, 0)

    @pl.when(st == n_steps - 1)
    def _():
        _softmax_update(_dot_nt(qz, kn_ref[0]) + biasn_ref[...], vn_ref[0], m_ref, l_ref, acc_ref, 0)
        lam = _lambda_value(lam_ref, lam_init)
        o = acc_ref[0] / l_ref[0]
        half = o.shape[0] // 2
        o = o[:half] - lam * o[half:]
        o_ref[0] = _head_norm(o, g_ref[...], lam_init).astype(o_ref.dtype)


def _attn_sample(page_table, qz, cache_k2, cache_v2, k_new, v_new, bias, bias_new, g_rows, lam_qk, lam_init):
    Bs, R, vd = qz.shape
    n_pages = page_table.shape[1]
    gp = math.gcd(n_pages, SAMPLE_PAGES_PER_STEP)
    n_steps = n_pages // gp
    rows = cache_k2.shape[1]
    rn = k_new.shape[1]
    pt = page_table.reshape(-1)

    def page_spec(g):
        return pl.BlockSpec((1, rows, vd), lambda b, s, pt_ref: (pt_ref[b * n_pages + s * gp + g], 0, 0))

    const2 = lambda b, s, pt_ref: (0, 0)
    seq3 = lambda b, s, pt_ref: (b, 0, 0)
    return pl.pallas_call(
        functools.partial(_attn_sample_kernel, gp=gp, n_steps=n_steps, lam_init=lam_init),
        out_shape=jax.ShapeDtypeStruct((Bs, R // 2, vd), BF16),
        grid_spec=pltpu.PrefetchScalarGridSpec(
            num_scalar_prefetch=1,
            grid=(Bs, n_steps),
            in_specs=[pl.BlockSpec(lam_qk.shape, const2), pl.BlockSpec((1, R, vd), seq3)]
            + [page_spec(g) for g in range(gp)] + [page_spec(g) for g in range(gp)]
            + [pl.BlockSpec((1, rn, vd), seq3), pl.BlockSpec((1, rn, vd), seq3),
               pl.BlockSpec(bias.shape, const2), pl.BlockSpec(bias_new.shape, const2),
               pl.BlockSpec(g_rows.shape, const2)],
            out_specs=pl.BlockSpec((1, R // 2, vd), seq3),
            scratch_shapes=[pltpu.VMEM((1, R, 1), F32), pltpu.VMEM((1, R, 1), F32),
                            pltpu.VMEM((1, R, vd), F32)],
        ),
        compiler_params=_cparams(("parallel", "arbitrary")),
        name="attn_sample",
    )(pt, lam_qk, qz, *([cache_k2] * gp), *([cache_v2] * gp), k_new, v_new, bias, bias_new, g_rows)


def _alibi_slopes(H):
    return 2.0 ** (-8.0 * np.arange(1, H + 1, dtype=np.float64) / H)


def _sample_bias_tables(H, Ts, page, n_pages):
    slopes = _alibi_slopes(H)
    pos0 = n_pages * page
    hq = np.tile(np.repeat(np.arange(H), Ts), 2)
    tq = np.tile(np.arange(Ts), 2 * H)
    key = np.repeat(np.arange(n_pages * page), H)
    hk = np.tile(np.arange(H), n_pages * page)
    dist = (pos0 + tq)[:, None] - key[None, :]
    bias = np.where(hq[:, None] == hk[None, :], -slopes[hq][:, None] * dist, NEG_INF)
    tn = np.repeat(np.arange(Ts), H)
    hn = np.tile(np.arange(H), Ts)
    dn = tq[:, None] - tn[None, :]
    bias_new = np.where((hq[:, None] == hn[None, :]) & (dn >= 0), -slopes[hq][:, None] * dn, NEG_INF)
    return jnp.asarray(bias, F32), jnp.asarray(bias_new, F32)


def kernel(x_prompt, x_sample, state_pool, cache_k, cache_v, page_table, p_prompt, p_sample, pool_w, pool_scale, ln1_g, ln1_b, ln2_g, ln2_b, ln_kv_g, ln_kv_b, w_kv, w_q, lam_qk, subln_g, w_o, dense_gu, dense_down, w_router, moe_gu, moe_down, ple_proj, ple_gate):
    B, T, D = x_prompt.shape
    Bs, Ts, _ = x_sample.shape
    depth = ln1_g.shape[0]
    n_a = pool_w.shape[0]
    n_pool, page, H, vd = cache_v.shape
    hd = vd // 2
    qk_width = H * 2 * hd
    n_e = w_router.shape[-1]
    n_pages = page_table.shape[1]
    pos0 = n_pages * page
    n_p, n_s = B * T, Bs * Ts
    n_all = n_p + n_s
    alpha = (2 * depth) ** 0.25
    row = lambda a: a.reshape(1, -1)

    pool_w_bf = pool_w.astype(BF16)
    dense_gu_bf, dense_down_bf = dense_gu.astype(BF16), dense_down.astype(BF16)
    moe_gu_bf, moe_down_bf = moe_gu.astype(BF16), moe_down.astype(BF16)
    w_kv_bf, w_q_bf, w_o_bf = w_kv.astype(BF16), w_q.astype(BF16), w_o.astype(BF16)
    proj_bf, gate_bf = ple_proj.astype(BF16), ple_gate.astype(BF16)
    wr_pad = jnp.pad(w_router, ((0, 0), (0, 0), (0, LANES - n_e)))

    p_all = jnp.concatenate([p_prompt.reshape(depth, n_p, -1), p_sample.reshape(depth, n_s, -1)], axis=1)
    slopes_np = _alibi_slopes(H)
    assert np.all((slopes_np * KPOS_RADIX).astype(BF16).astype(np.float64) == slopes_np * KPOS_RADIX)
    assert np.all(slopes_np.astype(BF16).astype(np.float64) == slopes_np)
    slopes = jnp.asarray(slopes_np, F32)
    bias, bias_new = _sample_bias_tables(H, Ts, page, n_pages)

    x_p = x_prompt
    x_s = x_sample
    x_all = None
    pool_p, pool_s = [], []
    kv = None
    for i in range(depth):
        if i < n_a:
            pool_p.append(x_p[:, T - POOL_BUF:])
            pool_s.append(jnp.concatenate([state_pool[i][:, Ts:], x_s], axis=1))
            x1_s = _pool_sample(state_pool[i].transpose(1, 0, 2), x_s.transpose(1, 0, 2), pool_w_bf[i],
                                row(pool_scale[i]), row(ln1_g[i]), row(ln1_b[i]), alpha, pos0)
            x1 = _pool_prompt(x_p, x1_s.transpose(1, 0, 2).reshape(n_s, D), pool_w_bf[i], row(pool_scale[i]),
                              row(ln1_g[i]), row(ln1_b[i]), alpha)
        else:
            jj = i - n_a
            lam_init = 0.8 - 0.6 * math.exp(-0.3 * i)
            if kv is None:
                kv = _shared_kv(x_all, row(ln_kv_g), row(ln_kv_b), w_kv_bf, qk_width)
            k_f, v_f, k_b, v_b = kv
            q = _proj(x_all, w_q_bf[jj], hd ** -0.5)
            o = _attn_prompt(q, k_b, v_b, slopes, lam_qk[jj], row(subln_g[jj]), B, T, H, lam_init,
                             PROMPT_HEADS_PER_STEP)
            q_s = q[n_p:].reshape(Bs, Ts, H, vd).transpose(0, 2, 1, 3).reshape(Bs, H * Ts, vd)
            lane = jnp.arange(vd)
            qz = jnp.concatenate([jnp.where(lane < hd, q_s, 0), jnp.where(lane >= hd, q_s, 0)], axis=1)
            g_rows = jnp.repeat(subln_g[jj].reshape(H, vd), Ts, axis=0)
            o_s = _attn_sample(page_table, qz, cache_k.reshape(n_pool, page * H, 2 * hd),
                               cache_v.reshape(n_pool, page * H, vd),
                               k_b[n_p:].reshape(Bs, Ts * H, 2 * hd), v_b[n_p:].reshape(Bs, Ts * H, vd),
                               bias, bias_new, g_rows, lam_qk[jj], lam_init)
            o_s = o_s.reshape(Bs, H, Ts, vd).transpose(0, 2, 1, 3).reshape(n_s, H * vd)
            x1 = _out_ln(o, o_s, w_o_bf[jj], x_all, row(ln1_g[i]), row(ln1_b[i]), alpha)
        if i % 2 == 0:
            x_all = _ffn(x1, dense_gu_bf[i // 2], dense_down_bf[i // 2],
                         row(ln2_g[i]), row(ln2_b[i]), p_all[i], proj_bf[i], gate_bf[i], alpha)
        else:
            x_all = _moe(x1, wr_pad[i // 2], n_e, moe_gu_bf[i // 2], moe_down_bf[i // 2],
                         row(ln2_g[i]), row(ln2_b[i]), p_all[i], proj_bf[i], gate_bf[i], alpha)
        x_p = x_all[:n_p].reshape(B, T, D)
        x_s = x_all[n_p:].reshape(Bs, Ts, D)

    k_f, v_f = kv[0], kv[1]
    return (x_p, x_s, jnp.stack(pool_p), jnp.stack(pool_s),
            k_f[:n_p].reshape(B, T, H, 2 * hd), v_f[:n_p].reshape(B, T, H, vd),
            k_f[n_p:].reshape(Bs, Ts, H, 2 * hd), v_f[n_p:].reshape(Bs, Ts, H, vd))
```

```python
import functools
import math

import numpy as np
import jax
import jax.numpy as jnp
from jax import lax
from jax.experimental import pallas as pl
from jax.experimental.pallas import tpu as pltpu

F32 = jnp.float32
BF16 = jnp.bfloat16

POOL_WINDOWS = (2, 4, 8, 16)
POOL_BUF = max(POOL_WINDOWS) - 1
POOL_HALO = 16
TOP_K = 2
LN_EPS = 1e-5
RMS_EPS = 1e-5
NEG_INF = -1e30
LANES = 128
VMEM_LIMIT = 56 * 1024 * 1024


def _cparams(sem):
    return pltpu.CompilerParams(dimension_semantics=sem, vmem_limit_bytes=VMEM_LIMIT)


def _layer_norm(x, g, b):
    mu = jnp.mean(x, axis=-1, keepdims=True)
    xc = x - mu
    var = jnp.mean(xc * xc, axis=-1, keepdims=True)
    return xc * lax.rsqrt(var + LN_EPS) * g + b


def _sigmoid(x):
    return 1.0 / (1.0 + jnp.exp(-x))


def _dot(a, b):
    return jnp.dot(a, b, preferred_element_type=F32)


def _dot_nt(a, b):
    return lax.dot_general(a, b, (((1,), (1,)), ((), ())), preferred_element_type=F32)


def _pick_tile(n, target):
    t = min(n, target)
    while n % t:
        t //= 2
    return t


def _pool_prompt_kernel(x_ref, halo_ref, tail_ref, w_ref, scale_ref, g_ref, b_ref, o_ref, ext_ref,
                        *, tm, nt, n_main, alpha):
    step = pl.program_id(0)

    @pl.when(step < n_main)
    def _():
        i = step % nt
        x = x_ref[0]
        ext_ref[0:POOL_HALO, :] = jnp.where(i > 0, halo_ref[0], 0.0)
        ext_ref[POOL_HALO:, :] = x
        gw = x.shape[-1] // len(POOL_WINDOWS)
        pos = lax.broadcasted_iota(jnp.int32, (tm, 1), 0) + i * tm
        ys = []
        for g, w in enumerate(POOL_WINDOWS):
            lo, hi = g * gw, (g + 1) * gw
            xs = x[:, lo:hi]
            s = xs
            for k in range(1, w):
                s = s + ext_ref[POOL_HALO - k:POOL_HALO - k + tm, lo:hi]
            cnt = jnp.minimum(w, pos + 1).astype(F32)
            d = s / cnt - xs
            ys.append(_dot(d.astype(BF16), w_ref[g]))
        y = jnp.concatenate(ys, axis=-1) * scale_ref[...]
        o_ref[...] = _layer_norm(alpha * x + y, g_ref[...], b_ref[...])

    @pl.when(step >= n_main)
    def _():
        o_ref[...] = tail_ref[...]


def _pool_prompt(x, tail, w_bf, scale, g, b, alpha):
    B, T, D = x.shape
    n_tail = tail.shape[0]
    tm = _pick_tile(math.gcd(T, n_tail), 512)
    nt = T // tm
    n_main = B * nt
    hb = tm // POOL_HALO
    G, gw = w_bf.shape[0], w_bf.shape[1]
    main = lambda s: jnp.minimum(s, n_main - 1)
    const2 = lambda s: (0, 0)
    return pl.pallas_call(
        functools.partial(_pool_prompt_kernel, tm=tm, nt=nt, n_main=n_main, alpha=alpha),
        out_shape=jax.ShapeDtypeStruct((B * T + n_tail, D), F32),
        grid=(n_main + n_tail // tm,),
        in_specs=[
            pl.BlockSpec((1, tm, D), lambda s: (main(s) // nt, main(s) % nt, 0)),
            pl.BlockSpec((1, POOL_HALO, D),
                         lambda s: (main(s) // nt, jnp.maximum((main(s) % nt) * hb - 1, 0), 0)),
            pl.BlockSpec((tm, D), lambda s: (jnp.maximum(s - n_main, 0), 0)),
            pl.BlockSpec((G, gw, gw), lambda s: (0, 0, 0)),
            pl.BlockSpec((1, D), const2),
            pl.BlockSpec((1, D), const2),
            pl.BlockSpec((1, D), const2),
        ],
        out_specs=pl.BlockSpec((tm, D), lambda s: (s, 0)),
        scratch_shapes=[pltpu.VMEM((tm + POOL_HALO, D), F32)],
        compiler_params=_cparams(("arbitrary",)),
        name="pool_prompt",
    )(x, x, tail, w_bf, scale, g, b)


def _pool_sample_kernel(buf_ref, x_ref, w_ref, scale_ref, g_ref, b_ref, o_ref, *, alpha, pos0):
    nb, ts = buf_ref.shape[0], x_ref.shape[0]
    rows = [buf_ref[r] for r in range(nb)] + [x_ref[t] for t in range(ts)]
    gw = rows[0].shape[-1] // len(POOL_WINDOWS)
    for t in range(ts):
        x = rows[nb + t]
        ys = []
        for g, w in enumerate(POOL_WINDOWS):
            lo, hi = g * gw, (g + 1) * gw
            xs = x[:, lo:hi]
            s = xs
            for k in range(1, w):
                s = s + rows[nb + t - k][:, lo:hi]
            d = s / float(min(w, pos0 + t + 1)) - xs
            ys.append(_dot(d.astype(BF16), w_ref[g]))
        y = jnp.concatenate(ys, axis=-1) * scale_ref[...]
        o_ref[t] = _layer_norm(alpha * x + y, g_ref[...], b_ref[...])


def _pool_sample(buf_t, x_t, w_bf, scale, g, b, alpha, pos0):
    nb, Bs, D = buf_t.shape
    ts = x_t.shape[0]
    bs = _pick_tile(Bs, 32)
    G, gw = w_bf.shape[0], w_bf.shape[1]
    return pl.pallas_call(
        functools.partial(_pool_sample_kernel, alpha=alpha, pos0=pos0),
        out_shape=jax.ShapeDtypeStruct((ts, Bs, D), F32),
        grid=(Bs // bs,),
        in_specs=[
            pl.BlockSpec((nb, bs, D), lambda i: (0, i, 0)),
            pl.BlockSpec((ts, bs, D), lambda i: (0, i, 0)),
            pl.BlockSpec((G, gw, gw), lambda i: (0, 0, 0)),
            pl.BlockSpec((1, D), lambda i: (0, 0)),
            pl.BlockSpec((1, D), lambda i: (0, 0)),
            pl.BlockSpec((1, D), lambda i: (0, 0)),
        ],
        out_specs=pl.BlockSpec((ts, bs, D), lambda i: (0, i, 0)),
        compiler_params=_cparams(("parallel",)),
        name="pool_sample",
    )(buf_t, x_t, w_bf, scale, g, b)


def _ln_ple(x, ff, g_ref, b_ref, p_ref, proj_ref, gw_ref, alpha):
    x2 = _layer_norm(alpha * x + ff, g_ref[...], b_ref[...])
    pp = _dot(p_ref[...].astype(BF16), proj_ref[...])
    gt = _dot(x2.astype(BF16), gw_ref[...])
    return x2 + pp * _sigmoid(gt)


def _swiglu_partial(xb, wg, wu, wd):
    h = _dot(xb, wg)
    u = _dot(xb, wu)
    return _dot((h * _sigmoid(h) * u).astype(BF16), wd)


def _ffn_kernel(x_ref, wg_ref, wu_ref, wd_ref, g_ref, b_ref, p_ref, proj_ref, gw_ref, o_ref, acc_ref, xb_ref,
                *, alpha):
    j = pl.program_id(1)

    @pl.when(j == 0)
    def _():
        acc_ref[...] = jnp.zeros_like(acc_ref)
        xb_ref[...] = x_ref[...].astype(BF16)

    acc_ref[...] += _swiglu_partial(xb_ref[...], wg_ref[...], wu_ref[...], wd_ref[...])

    @pl.when(j == pl.num_programs(1) - 1)
    def _():
        o_ref[...] = _ln_ple(x_ref[...], acc_ref[...], g_ref, b_ref, p_ref, proj_ref, gw_ref, alpha)


def _ffn_tile(F, target):
    best = LANES
    for t in range(LANES, min(F, target) + 1, LANES):
        if F % t == 0:
            best = t
    return best


def _ffn(x, w_gu, w_down, g, b, p, proj, gate_w, alpha):
    N, D = x.shape
    F = w_down.shape[0]
    P = p.shape[1]
    tm = _pick_tile(N, 512)
    tf = _ffn_tile(F, 1536)
    nf = F // tf
    return pl.pallas_call(
        functools.partial(_ffn_kernel, alpha=alpha),
        out_shape=jax.ShapeDtypeStruct((N, D), F32),
        grid=(N // tm, nf),
        in_specs=[
            pl.BlockSpec((tm, D), lambda i, j: (i, 0)),
            pl.BlockSpec((D, tf), lambda i, j: (0, j)),
            pl.BlockSpec((D, tf), lambda i, j: (0, nf + j)),
            pl.BlockSpec((tf, D), lambda i, j: (j, 0)),
            pl.BlockSpec((1, D), lambda i, j: (0, 0)),
            pl.BlockSpec((1, D), lambda i, j: (0, 0)),
            pl.BlockSpec((tm, P), lambda i, j: (i, 0)),
            pl.BlockSpec((P, D), lambda i, j: (0, 0)),
            pl.BlockSpec((D, D), lambda i, j: (0, 0)),
        ],
        out_specs=pl.BlockSpec((tm, D), lambda i, j: (i, 0)),
        scratch_shapes=[pltpu.VMEM((tm, D), F32), pltpu.VMEM((tm, D), BF16)],
        compiler_params=_cparams(("parallel", "arbitrary")),
        name="ffn_dense",
    )(x, w_gu, w_gu, w_down, g, b, p, proj, gate_w)


def _router_kernel(x_ref, wr_ref, o_ref, *, n_e):
    lg = jnp.dot(x_ref[...], wr_ref[...], preferred_element_type=F32, precision=lax.Precision.HIGHEST)
    lane = lax.broadcasted_iota(jnp.int32, lg.shape, 1)
    lg = jnp.where(lane < n_e, lg, -jnp.inf)
    v1 = jnp.max(lg, axis=-1, keepdims=True)
    i1 = jnp.min(jnp.where(lg == v1, lane, LANES), axis=-1, keepdims=True)
    lg2 = jnp.where(lane == i1, -jnp.inf, lg)
    v2 = jnp.max(lg2, axis=-1, keepdims=True)
    i2 = jnp.min(jnp.where(lg2 == v2, lane, LANES), axis=-1, keepdims=True)
    e2 = jnp.exp(v2 - v1)
    den = 1.0 + e2
    o_ref[...] = jnp.where(lane == i1, 1.0 / den, 0.0) + jnp.where(lane == i2, e2 / den, 0.0)


def _router(x, wr_pad, n_e):
    N, D = x.shape
    tm = _pick_tile(N, 512)
    return pl.pallas_call(
        functools.partial(_router_kernel, n_e=n_e),
        out_shape=jax.ShapeDtypeStruct((N, LANES), F32),
        grid=(N // tm,),
        in_specs=[pl.BlockSpec((tm, D), lambda i: (i, 0)),
                  pl.BlockSpec((D, LANES), lambda i: (0, 0))],
        out_specs=pl.BlockSpec((tm, LANES), lambda i: (i, 0)),
        compiler_params=_cparams(("parallel",)),
        name="router",
    )(x, wr_pad)


MOE_ROW_TILE = 1024
MOE_SUB_TILE = 512


def _route_plan(gates, n_e, tm):
    N = gates.shape[0]
    n_tiles = -(-TOP_K * N // tm) + n_e
    sel = gates[:, :n_e] > 0.0
    cnt = jnp.sum(sel, axis=0, dtype=jnp.int32)
    padded = (cnt + tm - 1) // tm * tm
    ends = jnp.cumsum(padded)
    rank = jnp.cumsum(sel.astype(jnp.int32), axis=0) - 1
    pos = (ends - padded)[None, :] + rank
    rows = n_tiles * tm
    pos_lo = jnp.minimum(jnp.min(jnp.where(sel, pos, rows), axis=1), rows - 1)
    pos_hi = jnp.max(jnp.where(sel, pos, -1), axis=1)
    pos_hi = jnp.where(pos_hi < 0, pos_lo, pos_hi)
    tile_start = jnp.arange(n_tiles, dtype=jnp.int32) * tm
    tile_e = jnp.minimum(jnp.sum(tile_start[:, None] >= ends[None, :], axis=1), n_e - 1).astype(jnp.int32)
    used = jnp.clip(jnp.take(ends - padded + cnt, tile_e) - tile_start, 0, tm)
    used = jnp.where(tile_start < ends[-1], used, 0).astype(jnp.int32)
    last_e = jnp.max(jnp.where(used > 0, tile_e, 0))
    tile_e = jnp.where(tile_start < ends[-1], tile_e, last_e)
    return pos_lo.astype(jnp.int32), pos_hi.astype(jnp.int32), tile_e, used, rows


def _row_copy(src_ref, src_row, dst_ref, dst_row, sem):
    return pltpu.make_async_copy(src_ref.at[pl.ds(src_row, 1)], dst_ref.at[pl.ds(dst_row, 1)], sem)


def _dispatch_kernel(lo_ref, hi_ref, x_ref, xs_in_ref, xs_ref, sem, *, tm):
    del xs_in_ref
    base = pl.program_id(0) * tm

    def start(r, c):
        _row_copy(x_ref, r, xs_ref, lo_ref[base + r], sem).start()
        _row_copy(x_ref, r, xs_ref, hi_ref[base + r], sem).start()
        return c

    lax.fori_loop(0, tm, start, 0)
    for _ in range(TOP_K):
        pltpu.make_async_copy(x_ref, xs_ref.at[pl.ds(0, tm)], sem).wait()


def _dispatch(pos_lo, pos_hi, x, rows):
    N, D = x.shape
    tm = _pick_tile(N, 256)
    return pl.pallas_call(
        functools.partial(_dispatch_kernel, tm=tm),
        out_shape=jax.ShapeDtypeStruct((rows, D), F32),
        grid_spec=pltpu.PrefetchScalarGridSpec(
            num_scalar_prefetch=2,
            grid=(N // tm,),
            in_specs=[pl.BlockSpec((tm, D), lambda i, lo, hi: (i, 0)),
                      pl.BlockSpec(memory_space=pl.ANY)],
            out_specs=pl.BlockSpec(memory_space=pl.ANY),
            scratch_shapes=[pltpu.SemaphoreType.DMA],
        ),
        input_output_aliases={3: 0},
        compiler_params=_cparams(("arbitrary",)),
        name="moe_dispatch",
    )(pos_lo, pos_hi, x, jnp.zeros((rows, D), F32))


def _experts_kernel(te_ref, used_ref, x_ref, wg_ref, wu_ref, wd_ref, o_ref, acc_ref, xb_ref, *, sub):
    r, j = pl.program_id(0), pl.program_id(1)
    used = used_ref[r]
    n_sub = x_ref.shape[0] // sub

    @pl.when(j == 0)
    def _():
        acc_ref[...] = jnp.zeros_like(acc_ref)
        xb_ref[...] = x_ref[...].astype(BF16)

    for s in range(n_sub):
        @pl.when(used > s * sub)
        def _():
            rs = pl.ds(s * sub, sub)
            acc_ref[rs, :] += _swiglu_partial(xb_ref[rs, :], wg_ref[0], wu_ref[0], wd_ref[0])

    @pl.when(j == pl.num_programs(1) - 1)
    def _():
        o_ref[...] = acc_ref[...]


def _experts(tile_e, used, xs, w_gu, w_down, tm):
    rows, D = xs.shape
    E, F = w_down.shape[0], w_down.shape[1]
    tf = _ffn_tile(F, 512)
    nf = F // tf

    def jmap(r, j, used_ref):
        return jnp.where(used_ref[r] > 0, j, nf - 1)

    return pl.pallas_call(
        functools.partial(_experts_kernel, sub=min(MOE_SUB_TILE, tm)),
        out_shape=jax.ShapeDtypeStruct((rows, D), F32),
        grid_spec=pltpu.PrefetchScalarGridSpec(
            num_scalar_prefetch=2,
            grid=(rows // tm, nf),
            in_specs=[
                pl.BlockSpec((tm, D), lambda r, j, te, us: (r, 0)),
                pl.BlockSpec((1, D, tf), lambda r, j, te, us: (te[r], 0, jmap(r, j, us))),
                pl.BlockSpec((1, D, tf), lambda r, j, te, us: (te[r], 0, nf + jmap(r, j, us))),
                pl.BlockSpec((1, tf, D), lambda r, j, te, us: (te[r], jmap(r, j, us), 0)),
            ],
            out_specs=pl.BlockSpec((tm, D), lambda r, j, te, us: (r, 0)),
            scratch_shapes=[pltpu.VMEM((tm, D), F32), pltpu.VMEM((tm, D), BF16)],
        ),
        compiler_params=_cparams(("parallel", "arbitrary")),
        name="moe_experts",
    )(tile_e, used, xs, w_gu, w_gu, w_down)


def _combine_kernel(lo_ref, hi_ref, x_ref, gates_ref, ys_ref, g_ref, b_ref, p_ref, proj_ref, gw_ref, o_ref,
                    ya_ref, yb_ref, sem, *, tm, alpha):
    base = pl.program_id(0) * tm

    def start(r, c):
        _row_copy(ys_ref, lo_ref[base + r], ya_ref, r, sem).start()
        _row_copy(ys_ref, hi_ref[base + r], yb_ref, r, sem).start()
        return c

    lax.fori_loop(0, tm, start, 0)
    gates = gates_ref[...]
    lane = lax.broadcasted_iota(jnp.int32, gates.shape, 1)
    sel = gates > 0.0
    e_lo = jnp.min(jnp.where(sel, lane, LANES), axis=-1, keepdims=True)
    e_hi = jnp.max(jnp.where(sel, lane, -1), axis=-1, keepdims=True)
    g_lo = jnp.sum(jnp.where(lane == e_lo, gates, 0.0), axis=-1, keepdims=True)
    g_hi = jnp.where(e_hi != e_lo, jnp.sum(jnp.where(lane == e_hi, gates, 0.0), axis=-1, keepdims=True), 0.0)
    pltpu.make_async_copy(ys_ref.at[pl.ds(0, tm)], ya_ref, sem).wait()
    pltpu.make_async_copy(ys_ref.at[pl.ds(0, tm)], yb_ref, sem).wait()
    ff = g_lo * ya_ref[...] + g_hi * yb_ref[...]
    o_ref[...] = _ln_ple(x_ref[...], ff, g_ref, b_ref, p_ref, proj_ref, gw_ref, alpha)


def _combine(pos_lo, pos_hi, x, gates, ys, g, b, p, proj, gate_w, alpha):
    N, D = x.shape
    P = p.shape[1]
    tm = _pick_tile(N, 256)
    row = lambda i, lo, hi: (i, 0)
    const = lambda i, lo, hi: (0, 0)
    return pl.pallas_call(
        functools.partial(_combine_kernel, tm=tm, alpha=alpha),
        out_shape=jax.ShapeDtypeStruct((N, D), F32),
        grid_spec=pltpu.PrefetchScalarGridSpec(
            num_scalar_prefetch=2,
            grid=(N // tm,),
            in_specs=[pl.BlockSpec((tm, D), row),
                      pl.BlockSpec((tm, LANES), row),
                      pl.BlockSpec(memory_space=pl.ANY),
                      pl.BlockSpec((1, D), const),
                      pl.BlockSpec((1, D), const),
                      pl.BlockSpec((tm, P), row),
                      pl.BlockSpec((P, D), const),
                      pl.BlockSpec((D, D), const)],
            out_specs=pl.BlockSpec((tm, D), row),
            scratch_shapes=[pltpu.VMEM((tm, D), F32), pltpu.VMEM((tm, D), F32), pltpu.SemaphoreType.DMA],
        ),
        compiler_params=_cparams(("arbitrary",)),
        name="moe_combine",
    )(pos_lo, pos_hi, x, gates, ys, g, b, p, proj, gate_w)


def _moe(x, wr_pad, n_e, w_gu, w_down, g, b, p, proj, gate_w, alpha):
    gates = _router(x, wr_pad, n_e)
    tm = MOE_ROW_TILE
    pos_lo, pos_hi, tile_e, used, rows = _route_plan(gates, n_e, tm)
    xs = _dispatch(pos_lo, pos_hi, x, rows)
    ys = _experts(tile_e, used, xs, w_gu, w_down, tm)
    return _combine(pos_lo, pos_hi, x, gates, ys, g, b, p, proj, gate_w, alpha)


def _kv_kernel(x_ref, g_ref, b_ref, w_ref, wvt_ref, k_ref, v_ref, kb_ref, vb_ref, vt_ref):
    xn = _layer_norm(x_ref[...], g_ref[...], b_ref[...]).astype(BF16)
    qk = k_ref.shape[-1]
    k = _dot(xn, w_ref[:, :qk])
    v = _dot(xn, w_ref[:, qk:])
    k_ref[...] = k
    v_ref[...] = v
    kb_ref[...] = k.astype(BF16)
    vb_ref[...] = v.astype(BF16)
    vt_ref[...] = _dot_nt(wvt_ref[...], xn).astype(BF16)


def _shared_kv(x, g, b, w_kv_bf, w_vt_bf, qk_width):
    N, D = x.shape
    W = w_kv_bf.shape[1]
    vw = W - qk_width
    tm = _pick_tile(N, 512)
    row = lambda i: (i, 0)
    const = lambda i: (0, 0)
    return pl.pallas_call(
        _kv_kernel,
        out_shape=(jax.ShapeDtypeStruct((N, qk_width), F32), jax.ShapeDtypeStruct((N, vw), F32),
                   jax.ShapeDtypeStruct((N, qk_width), BF16), jax.ShapeDtypeStruct((N, vw), BF16),
                   jax.ShapeDtypeStruct((vw, N), BF16)),
        grid=(N // tm,),
        in_specs=[pl.BlockSpec((tm, D), row), pl.BlockSpec((1, D), const), pl.BlockSpec((1, D), const),
                  pl.BlockSpec((D, W), const), pl.BlockSpec((vw, D), const)],
        out_specs=(pl.BlockSpec((tm, qk_width), row), pl.BlockSpec((tm, vw), row),
                   pl.BlockSpec((tm, qk_width), row), pl.BlockSpec((tm, vw), row),
                   pl.BlockSpec((vw, tm), lambda i: (0, i))),
        compiler_params=_cparams(("parallel",)),
        name="shared_kv",
    )(x, g, b, w_kv_bf, w_vt_bf)


def _proj_kernel(x_ref, w_ref, o_ref, *, scale):
    o_ref[...] = (_dot(x_ref[...].astype(BF16), w_ref[...]) * scale).astype(o_ref.dtype)


def _proj(x, w_bf, scale):
    N, D = x.shape
    W = w_bf.shape[1]
    tm = _pick_tile(N, 512)
    return pl.pallas_call(
        functools.partial(_proj_kernel, scale=scale),
        out_shape=jax.ShapeDtypeStruct((N, W), BF16),
        grid=(N // tm,),
        in_specs=[pl.BlockSpec((tm, D), lambda i: (i, 0)),
                  pl.BlockSpec((D, W), lambda i: (0, 0))],
        out_specs=pl.BlockSpec((tm, W), lambda i: (i, 0)),
        compiler_params=_cparams(("parallel",)),
        name="q_proj",
    )(x, w_bf)


def _out_ln_kernel(op_ref, os_ref, w_ref, x_ref, g_ref, b_ref, y_ref, *, n_main, alpha):
    o = jnp.where(pl.program_id(0) < n_main, op_ref[...], os_ref[...])
    y_ref[...] = _layer_norm(alpha * x_ref[...] + _dot(o, w_ref[...]), g_ref[...], b_ref[...])


def _out_ln(o_p, o_s, w_bf, x, g, b, alpha):
    N, D = x.shape
    W = o_p.shape[1]
    tm = _pick_tile(math.gcd(o_p.shape[0], o_s.shape[0]), 512)
    n_main = o_p.shape[0] // tm
    return pl.pallas_call(
        functools.partial(_out_ln_kernel, n_main=n_main, alpha=alpha),
        out_shape=jax.ShapeDtypeStruct((N, D), F32),
        grid=(N // tm,),
        in_specs=[pl.BlockSpec((tm, W), lambda i: (jnp.minimum(i, n_main - 1), 0)),
                  pl.BlockSpec((tm, W), lambda i: (jnp.maximum(i - n_main, 0), 0)),
                  pl.BlockSpec((W, D), lambda i: (0, 0)),
                  pl.BlockSpec((tm, D), lambda i: (i, 0)),
                  pl.BlockSpec((1, D), lambda i: (0, 0)),
                  pl.BlockSpec((1, D), lambda i: (0, 0))],
        out_specs=pl.BlockSpec((tm, D), lambda i: (i, 0)),
        compiler_params=_cparams(("parallel",)),
        name="attn_out_ln",
    )(o_p, o_s, w_bf, x, g, b)


def _lambda_value(lam_ref, lam_init):
    lq = lam_ref[...]
    s1 = jnp.sum(lq[0:1] * lq[1:2], axis=-1, keepdims=True)
    s2 = jnp.sum(lq[2:3] * lq[3:4], axis=-1, keepdims=True)
    return jnp.exp(s1) - jnp.exp(s2) + lam_init


def _head_norm(o, g, lam_init):
    return o * lax.rsqrt(jnp.mean(o * o, axis=-1, keepdims=True) + RMS_EPS) * g * (1.0 - lam_init)


def _softmax_update(s, v, m_ref, l_ref, acc_ref, idx):
    m_old = m_ref[idx]
    m_new = jnp.maximum(m_old, jnp.max(s, axis=-1, keepdims=True))
    a = jnp.exp(m_old - m_new)
    p = jnp.exp(s - m_new)
    l_ref[idx] = a * l_ref[idx] + jnp.sum(p, axis=-1, keepdims=True)
    acc_ref[idx] = a * acc_ref[idx] + _dot(p.astype(BF16), v)
    m_ref[idx] = m_new


KPOS_RADIX = 256
PROMPT_HEADS_PER_STEP = 2
STAT_ROWS = 8


def _attn_prompt_kernel(slopes_ref, lam_ref, q_ref, k_ref, vt_ref, g_ref, o_ref, qa_ref, m_ref, l_ref, acc_ref,
                        *, tq, hd, hp, lam_init):
    hg, i, j = pl.program_id(1), pl.program_id(2), pl.program_id(3)
    vd = 2 * hd

    @pl.when(j == 0)
    def _():
        m_ref[...] = jnp.full_like(m_ref, NEG_INF)
        l_ref[...] = jnp.zeros_like(l_ref)
        acc_ref[...] = jnp.zeros_like(acc_ref)
        lane = lax.broadcasted_iota(jnp.int32, (tq, vd), 1)
        for hh in range(hp):
            q = q_ref[:, hh * vd:(hh + 1) * vd]
            slope = slopes_ref[hg * hp + hh]
            aug = jnp.where(lane == 0, slope * KPOS_RADIX, jnp.where(lane == 1, slope, 0.0)).astype(BF16)
            qa_ref[hh, 0:tq, 0:vd] = jnp.where(lane < hd, q, jnp.zeros_like(q))
            qa_ref[hh, tq:, 0:vd] = jnp.where(lane >= hd, q, jnp.zeros_like(q))
            qa_ref[hh, 0:tq, vd:] = aug
            qa_ref[hh, tq:, vd:] = aug

    def block(diagonal):
        kpos = lax.broadcasted_iota(jnp.int32, (tq, vd), 0) + j * tq
        lane = lax.broadcasted_iota(jnp.int32, (tq, vd), 1)
        kaug = jnp.where(lane == 0, kpos // KPOS_RADIX, jnp.where(lane == 1, kpos % KPOS_RADIX, 0))
        kaug = kaug.astype(F32).astype(BF16)
        if diagonal:
            kr = lax.broadcasted_iota(jnp.int32, (tq, 2 * tq), 0)
            qc = lax.broadcasted_iota(jnp.int32, (tq, 2 * tq), 1)
            visible = jnp.where(qc >= tq, qc - tq, qc) >= kr
        for hh in range(hp):
            ka = jnp.concatenate([k_ref[:, hh * vd:(hh + 1) * vd], kaug], axis=1)
            s = _dot_nt(ka, qa_ref[hh])
            if diagonal:
                s = jnp.where(visible, s, NEG_INF)
            m_old = m_ref[hh]
            m_new = jnp.maximum(m_old, jnp.max(s, axis=0, keepdims=True))
            a = jnp.exp(m_old - m_new)
            p = jnp.exp(s - m_new[0:1])
            l_ref[hh] = a * l_ref[hh] + jnp.sum(p, axis=0, keepdims=True)
            acc_ref[hh] = a[0:1] * acc_ref[hh] + _dot(vt_ref[hh * vd:(hh + 1) * vd, :], p.astype(BF16))
            m_ref[hh] = m_new

    @pl.when(j < i)
    def _():
        block(False)

    @pl.when(j == i)
    def _():
        block(True)
        lam = _lambda_value(lam_ref, lam_init)
        for hh in range(hp):
            ot = acc_ref[hh] / l_ref[hh][0:1]
            ot = ot[:, :tq] - lam * ot[:, tq:]
            ot = ot * lax.rsqrt(jnp.mean(ot * ot, axis=0, keepdims=True) + RMS_EPS)
            o = ot.T * g_ref[:, hh * vd:(hh + 1) * vd] * (1.0 - lam_init)
            o_ref[:, hh * vd:(hh + 1) * vd] = o.astype(o_ref.dtype)


def _attn_prompt(q, k, vt, slopes, lam_qk, subln_g, B, T, H, lam_init, hp):
    W = q.shape[1]
    vd = W // H
    hd = vd // 2
    tq = _pick_tile(T, 512)
    nq = T // tq
    assert T <= KPOS_RADIX * KPOS_RADIX and H % hp == 0
    return pl.pallas_call(
        functools.partial(_attn_prompt_kernel, tq=tq, hd=hd, hp=hp, lam_init=lam_init),
        out_shape=jax.ShapeDtypeStruct((B * T, W), BF16),
        grid=(B, H // hp, nq, nq),
        in_specs=[
            pl.BlockSpec(memory_space=pltpu.SMEM),
            pl.BlockSpec(lam_qk.shape, lambda b, h, i, j: (0, 0)),
            pl.BlockSpec((tq, hp * vd), lambda b, h, i, j: (b * nq + i, h)),
            pl.BlockSpec((tq, hp * vd), lambda b, h, i, j: (b * nq + jnp.minimum(j, i), h)),
            pl.BlockSpec((hp * vd, tq), lambda b, h, i, j: (h, b * nq + jnp.minimum(j, i))),
            pl.BlockSpec((1, hp * vd), lambda b, h, i, j: (0, h)),
        ],
        out_specs=pl.BlockSpec((tq, hp * vd), lambda b, h, i, j: (b * nq + i, h)),
        scratch_shapes=[pltpu.VMEM((hp, 2 * tq, 2 * vd), BF16), pltpu.VMEM((hp, STAT_ROWS, 2 * tq), F32),
                        pltpu.VMEM((hp, STAT_ROWS, 2 * tq), F32), pltpu.VMEM((hp, vd, 2 * tq), F32)],
        compiler_params=_cparams(("parallel", "parallel", "parallel", "arbitrary")),
        name="attn_prompt",
    )(slopes, lam_qk, q, k, vt, subln_g)


SAMPLE_PAGES_PER_STEP = 8


def _attn_sample_kernel(pt_ref, lam_ref, qz_ref, *refs, gp, n_steps, lam_init):
    k_refs, v_refs = refs[:gp], refs[gp:2 * gp]
    kn_ref, vn_ref, bias_ref, biasn_ref, g_ref, o_ref, m_ref, l_ref, acc_ref = refs[2 * gp:]
    st = pl.program_id(1)
    rows = k_refs[0].shape[1]

    @pl.when(st == 0)
    def _():
        m_ref[...] = jnp.full_like(m_ref, NEG_INF)
        l_ref[...] = jnp.zeros_like(l_ref)
        acc_ref[...] = jnp.zeros_like(acc_ref)

    qz = qz_ref[0]
    for step in range(n_steps):
        @pl.when(st == step)
        def _():
            ss = []
            for g in range(gp):
                pg = step * gp + g
                ss.append(_dot_nt(qz, k_refs[g][0].astype(BF16)) + bias_ref[:, pg * rows:(pg + 1) * rows])
            vb = jnp.concatenate([v_refs[g][0].astype(BF16) for g in range(gp)], axis=0)
            _softmax_update(jnp.concatenate(ss, axis=1), vb, m_ref, l_ref, acc_ref, 0)

    @pl.when(st == n_steps - 1)
    def _():
        _softmax_update(_dot_nt(qz, kn_ref[0]) + biasn_ref[...], vn_ref[0], m_ref, l_ref, acc_ref, 0)
        lam = _lambda_value(lam_ref, lam_init)
        o = acc_ref[0] / l_ref[0]
        half = o.shape[0] // 2
        o = o[:half] - lam * o[half:]
        o_ref[0] = _head_norm(o, g_ref[...], lam_init).astype(o_ref.dtype)


def _attn_sample(page_table, qz, cache_k2, cache_v2, k_new, v_new, bias, bias_new, g_rows, lam_qk, lam_init):
    Bs, R, vd = qz.shape
    n_pages = page_table.shape[1]
    gp = math.gcd(n_pages, SAMPLE_PAGES_PER_STEP)
    n_steps = n_pages // gp
    rows = cache_k2.shape[1]
    rn = k_new.shape[1]
    pt = page_table.reshape(-1)

    def page_spec(g):
        return pl.BlockSpec((1, rows, vd), lambda b, s, pt_ref: (pt_ref[b * n_pages + s * gp + g], 0, 0))

    const2 = lambda b, s, pt_ref: (0, 0)
    seq3 = lambda b, s, pt_ref: (b, 0, 0)
    return pl.pallas_call(
        functools.partial(_attn_sample_kernel, gp=gp, n_steps=n_steps, lam_init=lam_init),
        out_shape=jax.ShapeDtypeStruct((Bs, R // 2, vd), BF16),
        grid_spec=pltpu.PrefetchScalarGridSpec(
            num_scalar_prefetch=1,
            grid=(Bs, n_steps),
            in_specs=[pl.BlockSpec(lam_qk.shape, const2), pl.BlockSpec((1, R, vd), seq3)]
            + [page_spec(g) for g in range(gp)] + [page_spec(g) for g in range(gp)]
            + [pl.BlockSpec((1, rn, vd), seq3), pl.BlockSpec((1, rn, vd), seq3),
               pl.BlockSpec(bias.shape, const2), pl.BlockSpec(bias_new.shape, const2),
               pl.BlockSpec(g_rows.shape, const2)],
            out_specs=pl.BlockSpec((1, R // 2, vd), seq3),
            scratch_shapes=[pltpu.VMEM((1, R, 1), F32), pltpu.VMEM((1, R, 1), F32),
                            pltpu.VMEM((1, R, vd), F32)],
        ),
        compiler_params=_cparams(("parallel", "arbitrary")),
        name="attn_sample",
    )(pt, lam_qk, qz, *([cache_k2] * gp), *([cache_v2] * gp), k_new, v_new, bias, bias_new, g_rows)


def _alibi_slopes(H):
    return 2.0 ** (-8.0 * np.arange(1, H + 1, dtype=np.float64) / H)


def _sample_bias_tables(H, Ts, page, n_pages):
    slopes = _alibi_slopes(H)
    pos0 = n_pages * page
    hq = np.tile(np.repeat(np.arange(H), Ts), 2)
    tq = np.tile(np.arange(Ts), 2 * H)
    key = np.repeat(np.arange(n_pages * page), H)
    hk = np.tile(np.arange(H), n_pages * page)
    dist = (pos0 + tq)[:, None] - key[None, :]
    bias = np.where(hq[:, None] == hk[None, :], -slopes[hq][:, None] * dist, NEG_INF)
    tn = np.repeat(np.arange(Ts), H)
    hn = np.tile(np.arange(H), Ts)
    dn = tq[:, None] - tn[None, :]
    bias_new = np.where((hq[:, None] == hn[None, :]) & (dn >= 0), -slopes[hq][:, None] * dn, NEG_INF)
    return jnp.asarray(bias, F32), jnp.asarray(bias_new, F32)


def kernel(x_prompt, x_sample, state_pool, cache_k, cache_v, page_table, p_prompt, p_sample, pool_w, pool_scale, ln1_g, ln1_b, ln2_g, ln2_b, ln_kv_g, ln_kv_b, w_kv, w_q, lam_qk, subln_g, w_o, dense_gu, dense_down, w_router, moe_gu, moe_down, ple_proj, ple_gate):
    B, T, D = x_prompt.shape
    Bs, Ts, _ = x_sample.shape
    depth = ln1_g.shape[0]
    n_a = pool_w.shape[0]
    n_pool, page, H, vd = cache_v.shape
    hd = vd // 2
    qk_width = H * 2 * hd
    n_e = w_router.shape[-1]
    n_pages = page_table.shape[1]
    pos0 = n_pages * page
    n_p, n_s = B * T, Bs * Ts
    n_all = n_p + n_s
    alpha = (2 * depth) ** 0.25
    row = lambda a: a.reshape(1, -1)

    pool_w_bf = pool_w.astype(BF16)
    dense_gu_bf, dense_down_bf = dense_gu.astype(BF16), dense_down.astype(BF16)
    moe_gu_bf, moe_down_bf = moe_gu.astype(BF16), moe_down.astype(BF16)
    w_kv_bf, w_q_bf, w_o_bf = w_kv.astype(BF16), w_q.astype(BF16), w_o.astype(BF16)
    w_vt_bf = w_kv[:, qk_width:].T.astype(BF16)
    proj_bf, gate_bf = ple_proj.astype(BF16), ple_gate.astype(BF16)
    wr_pad = jnp.pad(w_router, ((0, 0), (0, 0), (0, LANES - n_e)))

    p_all = jnp.concatenate([p_prompt.reshape(depth, n_p, -1), p_sample.reshape(depth, n_s, -1)], axis=1)
    slopes_np = _alibi_slopes(H)
    assert np.all((slopes_np * KPOS_RADIX).astype(BF16).astype(np.float64) == slopes_np * KPOS_RADIX)
    assert np.all(slopes_np.astype(BF16).astype(np.float64) == slopes_np)
    slopes = jnp.asarray(slopes_np, F32)
    bias, bias_new = _sample_bias_tables(H, Ts, page, n_pages)

    x_p = x_prompt
    x_s = x_sample
    x_all = None
    pool_p, pool_s = [], []
    kv = None
    for i in range(depth):
        if i < n_a:
            pool_p.append(x_p[:, T - POOL_BUF:])
            pool_s.append(jnp.concatenate([state_pool[i][:, Ts:], x_s], axis=1))
            x1_s = _pool_sample(state_pool[i].transpose(1, 0, 2), x_s.transpose(1, 0, 2), pool_w_bf[i],
                                row(pool_scale[i]), row(ln1_g[i]), row(ln1_b[i]), alpha, pos0)
            x1 = _pool_prompt(x_p, x1_s.transpose(1, 0, 2).reshape(n_s, D), pool_w_bf[i], row(pool_scale[i]),
                              row(ln1_g[i]), row(ln1_b[i]), alpha)
        else:
            jj = i - n_a
            lam_init = 0.8 - 0.6 * math.exp(-0.3 * i)
            if kv is None:
                kv = _shared_kv(x_all, row(ln_kv_g), row(ln_kv_b), w_kv_bf, w_vt_bf, qk_width)
            k_f, v_f, k_b, v_b, v_t = kv
            q = _proj(x_all, w_q_bf[jj], hd ** -0.5)
            o = _attn_prompt(q, k_b, v_t, slopes, lam_qk[jj], row(subln_g[jj]), B, T, H, lam_init,
                             PROMPT_HEADS_PER_STEP)
            q_s = q[n_p:].reshape(Bs, Ts, H, vd).transpose(0, 2, 1, 3).reshape(Bs, H * Ts, vd)
            lane = jnp.arange(vd)
            qz = jnp.concatenate([jnp.where(lane < hd, q_s, 0), jnp.where(lane >= hd, q_s, 0)], axis=1)
            g_rows = jnp.repeat(subln_g[jj].reshape(H, vd), Ts, axis=0)
            o_s = _attn_sample(page_table, qz, cache_k.reshape(n_pool, page * H, 2 * hd),
                               cache_v.reshape(n_pool, page * H, vd),
                               k_b[n_p:].reshape(Bs, Ts * H, 2 * hd), v_b[n_p:].reshape(Bs, Ts * H, vd),
                               bias, bias_new, g_rows, lam_qk[jj], lam_init)
            o_s = o_s.reshape(Bs, H, Ts, vd).transpose(0, 2, 1, 3).reshape(n_s, H * vd)
            x1 = _out_ln(o, o_s, w_o_bf[jj], x_all, row(ln1_g[i]), row(ln1_b[i]), alpha)
        if i % 2 == 0:
            x_all = _ffn(x1, dense_gu_bf[i // 2], dense_down_bf[i // 2],
                         row(ln2_g[i]), row(ln2_b[i]), p_all[i], proj_bf[i], gate_bf[i], alpha)
        else:
            x_all = _moe(x1, wr_pad[i // 2], n_e, moe_gu_bf[i // 2], moe_down_bf[i // 2],
                         row(ln2_g[i]), row(ln2_b[i]), p_all[i], proj_bf[i], gate_bf[i], alpha)
        x_p = x_all[:n_p].reshape(B, T, D)
        x_s = x_all[n_p:].reshape(Bs, Ts, D)

    k_f, v_f = kv[0], kv[1]
    return (x_p, x_s, jnp.stack(pool_p), jnp.stack(pool_s),
            k_f[:n_p].reshape(B, T, H, 2 * hd), v_f[:n_p].reshape(B, T, H, vd),
            k_f[n_p:].reshape(Bs, Ts, H, 2 * hd), v_f[n_p:].reshape(Bs, Ts, H, vd))
```

```python
import functools
import math

import numpy as np
import jax
import jax.numpy as jnp
from jax import lax
from jax.experimental import pallas as pl
from jax.experimental.pallas import tpu as pltpu

F32 = jnp.float32
BF16 = jnp.bfloat16

POOL_WINDOWS = (2, 4, 8, 16)
POOL_BUF = max(POOL_WINDOWS) - 1
POOL_HALO = 16
TOP_K = 2
LN_EPS = 1e-5
RMS_EPS = 1e-5
NEG_INF = -1e30
LANES = 128
VMEM_LIMIT = 56 * 1024 * 1024


def _cparams(sem):
    return pltpu.CompilerParams(dimension_semantics=sem, vmem_limit_bytes=VMEM_LIMIT)


def _layer_norm(x, g, b):
    mu = jnp.mean(x, axis=-1, keepdims=True)
    xc = x - mu
    var = jnp.mean(xc * xc, axis=-1, keepdims=True)
    return xc * lax.rsqrt(var + LN_EPS) * g + b


def _sigmoid(x):
    return 1.0 / (1.0 + jnp.exp(-x))


def _dot(a, b):
    return jnp.dot(a, b, preferred_element_type=F32)


def _dot_nt(a, b):
    return lax.dot_general(a, b, (((1,), (1,)), ((), ())), preferred_element_type=F32)


def _pick_tile(n, target):
    t = min(n, target)
    while n % t:
        t //= 2
    return t


def _pool_prompt_kernel(x_ref, halo_ref, tail_ref, w_ref, scale_ref, g_ref, b_ref, o_ref, ext_ref,
                        *, tm, nt, n_main, alpha):
    step = pl.program_id(0)

    @pl.when(step < n_main)
    def _():
        i = step % nt
        x = x_ref[...]
        ext_ref[0:POOL_HALO, :] = jnp.where(i > 0, halo_ref[...], 0.0)
        ext_ref[POOL_HALO:, :] = x
        gw = x.shape[-1] // len(POOL_WINDOWS)
        pos = lax.broadcasted_iota(jnp.int32, (tm, 1), 0) + i * tm
        ys = []
        for g, w in enumerate(POOL_WINDOWS):
            lo, hi = g * gw, (g + 1) * gw
            xs = x[:, lo:hi]
            s = xs
            for k in range(1, w):
                s = s + ext_ref[POOL_HALO - k:POOL_HALO - k + tm, lo:hi]
            cnt = jnp.minimum(w, pos + 1).astype(F32)
            d = s / cnt - xs
            ys.append(_dot(d.astype(BF16), w_ref[g]))
        y = jnp.concatenate(ys, axis=-1) * scale_ref[...]
        o_ref[...] = _layer_norm(alpha * x + y, g_ref[...], b_ref[...])

    @pl.when(step >= n_main)
    def _():
        o_ref[...] = tail_ref[...]


def _pool_prompt(x, B, T, tail, w_bf, scale, g, b, alpha):
    D = x.shape[1]
    n_tail = tail.shape[0]
    tm = _pick_tile(math.gcd(T, n_tail), 512)
    nt = T // tm
    n_main = B * nt
    hb = tm // POOL_HALO
    G, gw = w_bf.shape[0], w_bf.shape[1]
    main = lambda s: jnp.minimum(s, n_main - 1)
    const2 = lambda s: (0, 0)
    return pl.pallas_call(
        functools.partial(_pool_prompt_kernel, tm=tm, nt=nt, n_main=n_main, alpha=alpha),
        out_shape=jax.ShapeDtypeStruct((B * T + n_tail, D), F32),
        grid=(n_main + n_tail // tm,),
        in_specs=[
            pl.BlockSpec((tm, D), lambda s: (main(s), 0)),
            pl.BlockSpec((POOL_HALO, D), lambda s: (jnp.maximum(main(s) * hb - 1, 0), 0)),
            pl.BlockSpec((tm, D), lambda s: (jnp.maximum(s - n_main, 0), 0)),
            pl.BlockSpec((G, gw, gw), lambda s: (0, 0, 0)),
            pl.BlockSpec((1, D), const2),
            pl.BlockSpec((1, D), const2),
            pl.BlockSpec((1, D), const2),
        ],
        out_specs=pl.BlockSpec((tm, D), lambda s: (s, 0)),
        scratch_shapes=[pltpu.VMEM((tm + POOL_HALO, D), F32)],
        compiler_params=_cparams(("arbitrary",)),
        name="pool_prompt",
    )(x, x, tail, w_bf, scale, g, b)


def _pool_sample_kernel(buf_ref, x_ref, w_ref, scale_ref, g_ref, b_ref, o_ref, *, alpha, pos0):
    nb, ts = buf_ref.shape[0], x_ref.shape[0]
    rows = [buf_ref[r] for r in range(nb)] + [x_ref[t] for t in range(ts)]
    gw = rows[0].shape[-1] // len(POOL_WINDOWS)
    for t in range(ts):
        x = rows[nb + t]
        ys = []
        for g, w in enumerate(POOL_WINDOWS):
            lo, hi = g * gw, (g + 1) * gw
            xs = x[:, lo:hi]
            s = xs
            for k in range(1, w):
                s = s + rows[nb + t - k][:, lo:hi]
            d = s / float(min(w, pos0 + t + 1)) - xs
            ys.append(_dot(d.astype(BF16), w_ref[g]))
        y = jnp.concatenate(ys, axis=-1) * scale_ref[...]
        o_ref[t] = _layer_norm(alpha * x + y, g_ref[...], b_ref[...])


def _pool_sample(buf_t, x_t, w_bf, scale, g, b, alpha, pos0):
    nb, Bs, D = buf_t.shape
    ts = x_t.shape[0]
    bs = _pick_tile(Bs, 32)
    G, gw = w_bf.shape[0], w_bf.shape[1]
    return pl.pallas_call(
        functools.partial(_pool_sample_kernel, alpha=alpha, pos0=pos0),
        out_shape=jax.ShapeDtypeStruct((ts, Bs, D), F32),
        grid=(Bs // bs,),
        in_specs=[
            pl.BlockSpec((nb, bs, D), lambda i: (0, i, 0)),
            pl.BlockSpec((ts, bs, D), lambda i: (0, i, 0)),
            pl.BlockSpec((G, gw, gw), lambda i: (0, 0, 0)),
            pl.BlockSpec((1, D), lambda i: (0, 0)),
            pl.BlockSpec((1, D), lambda i: (0, 0)),
            pl.BlockSpec((1, D), lambda i: (0, 0)),
        ],
        out_specs=pl.BlockSpec((ts, bs, D), lambda i: (0, i, 0)),
        compiler_params=_cparams(("parallel",)),
        name="pool_sample",
    )(buf_t, x_t, w_bf, scale, g, b)


def _ln_ple(x, ff, g_ref, b_ref, p_ref, proj_ref, gw_ref, alpha):
    x2 = _layer_norm(alpha * x + ff, g_ref[...], b_ref[...])
    pp = _dot(p_ref[...].astype(BF16), proj_ref[...])
    gt = _dot(x2.astype(BF16), gw_ref[...])
    return x2 + pp * _sigmoid(gt)


def _swiglu_partial(xb, wg, wu, wd):
    h = _dot(xb, wg)
    u = _dot(xb, wu)
    return _dot((h * _sigmoid(h) * u).astype(BF16), wd)


def _ffn_kernel(x_ref, wg_ref, wu_ref, wd_ref, g_ref, b_ref, p_ref, proj_ref, gw_ref, o_ref, acc_ref, xb_ref,
                *, alpha):
    j = pl.program_id(1)

    @pl.when(j == 0)
    def _():
        acc_ref[...] = jnp.zeros_like(acc_ref)
        xb_ref[...] = x_ref[...].astype(BF16)

    acc_ref[...] += _swiglu_partial(xb_ref[...], wg_ref[...], wu_ref[...], wd_ref[...])

    @pl.when(j == pl.num_programs(1) - 1)
    def _():
        o_ref[...] = _ln_ple(x_ref[...], acc_ref[...], g_ref, b_ref, p_ref, proj_ref, gw_ref, alpha)


def _ffn_tile(F, target):
    best = LANES
    for t in range(LANES, min(F, target) + 1, LANES):
        if F % t == 0:
            best = t
    return best


def _ffn(x, w_gu, w_down, g, b, p, proj, gate_w, alpha):
    N, D = x.shape
    F = w_down.shape[0]
    P = p.shape[1]
    tm = _pick_tile(N, 512)
    tf = _ffn_tile(F, 1536)
    nf = F // tf
    return pl.pallas_call(
        functools.partial(_ffn_kernel, alpha=alpha),
        out_shape=jax.ShapeDtypeStruct((N, D), F32),
        grid=(N // tm, nf),
        in_specs=[
            pl.BlockSpec((tm, D), lambda i, j: (i, 0)),
            pl.BlockSpec((D, tf), lambda i, j: (0, j)),
            pl.BlockSpec((D, tf), lambda i, j: (0, nf + j)),
            pl.BlockSpec((tf, D), lambda i, j: (j, 0)),
            pl.BlockSpec((1, D), lambda i, j: (0, 0)),
            pl.BlockSpec((1, D), lambda i, j: (0, 0)),
            pl.BlockSpec((tm, P), lambda i, j: (i, 0)),
            pl.BlockSpec((P, D), lambda i, j: (0, 0)),
            pl.BlockSpec((D, D), lambda i, j: (0, 0)),
        ],
        out_specs=pl.BlockSpec((tm, D), lambda i, j: (i, 0)),
        scratch_shapes=[pltpu.VMEM((tm, D), F32), pltpu.VMEM((tm, D), BF16)],
        compiler_params=_cparams(("parallel", "arbitrary")),
        name="ffn_dense",
    )(x, w_gu, w_gu, w_down, g, b, p, proj, gate_w)


def _router_kernel(x_ref, wr_ref, o_ref, *, n_e):
    lg = jnp.dot(x_ref[...], wr_ref[...], preferred_element_type=F32, precision=lax.Precision.HIGHEST)
    lane = lax.broadcasted_iota(jnp.int32, lg.shape, 1)
    lg = jnp.where(lane < n_e, lg, -jnp.inf)
    v1 = jnp.max(lg, axis=-1, keepdims=True)
    i1 = jnp.min(jnp.where(lg == v1, lane, LANES), axis=-1, keepdims=True)
    lg2 = jnp.where(lane == i1, -jnp.inf, lg)
    v2 = jnp.max(lg2, axis=-1, keepdims=True)
    i2 = jnp.min(jnp.where(lg2 == v2, lane, LANES), axis=-1, keepdims=True)
    e2 = jnp.exp(v2 - v1)
    den = 1.0 + e2
    o_ref[...] = jnp.where(lane == i1, 1.0 / den, 0.0) + jnp.where(lane == i2, e2 / den, 0.0)


def _router(x, wr_pad, n_e):
    N, D = x.shape
    tm = _pick_tile(N, 512)
    return pl.pallas_call(
        functools.partial(_router_kernel, n_e=n_e),
        out_shape=jax.ShapeDtypeStruct((N, LANES), F32),
        grid=(N // tm,),
        in_specs=[pl.BlockSpec((tm, D), lambda i: (i, 0)),
                  pl.BlockSpec((D, LANES), lambda i: (0, 0))],
        out_specs=pl.BlockSpec((tm, LANES), lambda i: (i, 0)),
        compiler_params=_cparams(("parallel",)),
        name="router",
    )(x, wr_pad)


MOE_ROW_TILE = 1024
MOE_SUB_TILE = 512


def _route_plan(gates, n_e, tm):
    N = gates.shape[0]
    n_tiles = -(-TOP_K * N // tm) + n_e
    sel = gates[:, :n_e] > 0.0
    cnt = jnp.sum(sel, axis=0, dtype=jnp.int32)
    padded = (cnt + tm - 1) // tm * tm
    ends = jnp.cumsum(padded)
    rank = jnp.cumsum(sel.astype(jnp.int32), axis=0) - 1
    pos = (ends - padded)[None, :] + rank
    rows = n_tiles * tm
    pos_lo = jnp.minimum(jnp.min(jnp.where(sel, pos, rows), axis=1), rows - 1)
    pos_hi = jnp.max(jnp.where(sel, pos, -1), axis=1)
    pos_hi = jnp.where(pos_hi < 0, pos_lo, pos_hi)
    tile_start = jnp.arange(n_tiles, dtype=jnp.int32) * tm
    tile_e = jnp.minimum(jnp.sum(tile_start[:, None] >= ends[None, :], axis=1), n_e - 1).astype(jnp.int32)
    used = jnp.clip(jnp.take(ends - padded + cnt, tile_e) - tile_start, 0, tm)
    used = jnp.where(tile_start < ends[-1], used, 0).astype(jnp.int32)
    last_e = jnp.max(jnp.where(used > 0, tile_e, 0))
    tile_e = jnp.where(tile_start < ends[-1], tile_e, last_e)
    return pos_lo.astype(jnp.int32), pos_hi.astype(jnp.int32), tile_e, used, rows


def _row_copy(src_ref, src_row, dst_ref, dst_row, sem):
    return pltpu.make_async_copy(src_ref.at[pl.ds(src_row, 1)], dst_ref.at[pl.ds(dst_row, 1)], sem)


def _dispatch_kernel(lo_ref, hi_ref, x_ref, xs_in_ref, xs_ref, sem, *, tm):
    del xs_in_ref
    base = pl.program_id(0) * tm

    def start(r, c):
        _row_copy(x_ref, r, xs_ref, lo_ref[base + r], sem).start()
        _row_copy(x_ref, r, xs_ref, hi_ref[base + r], sem).start()
        return c

    lax.fori_loop(0, tm, start, 0)
    for _ in range(TOP_K):
        pltpu.make_async_copy(x_ref, xs_ref.at[pl.ds(0, tm)], sem).wait()


def _dispatch(pos_lo, pos_hi, x, rows):
    N, D = x.shape
    tm = _pick_tile(N, 256)
    return pl.pallas_call(
        functools.partial(_dispatch_kernel, tm=tm),
        out_shape=jax.ShapeDtypeStruct((rows, D), F32),
        grid_spec=pltpu.PrefetchScalarGridSpec(
            num_scalar_prefetch=2,
            grid=(N // tm,),
            in_specs=[pl.BlockSpec((tm, D), lambda i, lo, hi: (i, 0)),
                      pl.BlockSpec(memory_space=pl.ANY)],
            out_specs=pl.BlockSpec(memory_space=pl.ANY),
            scratch_shapes=[pltpu.SemaphoreType.DMA],
        ),
        input_output_aliases={3: 0},
        compiler_params=_cparams(("arbitrary",)),
        name="moe_dispatch",
    )(pos_lo, pos_hi, x, jnp.zeros((rows, D), F32))


def _experts_kernel(te_ref, used_ref, x_ref, wg_ref, wu_ref, wd_ref, o_ref, acc_ref, xb_ref, *, sub):
    r, j = pl.program_id(0), pl.program_id(1)
    used = used_ref[r]
    n_sub = x_ref.shape[0] // sub

    @pl.when(j == 0)
    def _():
        acc_ref[...] = jnp.zeros_like(acc_ref)
        xb_ref[...] = x_ref[...].astype(BF16)

    for s in range(n_sub):
        @pl.when(used > s * sub)
        def _():
            rs = pl.ds(s * sub, sub)
            acc_ref[rs, :] += _swiglu_partial(xb_ref[rs, :], wg_ref[0], wu_ref[0], wd_ref[0])

    @pl.when(j == pl.num_programs(1) - 1)
    def _():
        o_ref[...] = acc_ref[...]


def _experts(tile_e, used, xs, w_gu, w_down, tm):
    rows, D = xs.shape
    E, F = w_down.shape[0], w_down.shape[1]
    tf = _ffn_tile(F, 512)
    nf = F // tf

    def jmap(r, j, used_ref):
        return jnp.where(used_ref[r] > 0, j, nf - 1)

    return pl.pallas_call(
        functools.partial(_experts_kernel, sub=min(MOE_SUB_TILE, tm)),
        out_shape=jax.ShapeDtypeStruct((rows, D), F32),
        grid_spec=pltpu.PrefetchScalarGridSpec(
            num_scalar_prefetch=2,
            grid=(rows // tm, nf),
            in_specs=[
                pl.BlockSpec((tm, D), lambda r, j, te, us: (r, 0)),
                pl.BlockSpec((1, D, tf), lambda r, j, te, us: (te[r], 0, jmap(r, j, us))),
                pl.BlockSpec((1, D, tf), lambda r, j, te, us: (te[r], 0, nf + jmap(r, j, us))),
                pl.BlockSpec((1, tf, D), lambda r, j, te, us: (te[r], jmap(r, j, us), 0)),
            ],
            out_specs=pl.BlockSpec((tm, D), lambda r, j, te, us: (r, 0)),
            scratch_shapes=[pltpu.VMEM((tm, D), F32), pltpu.VMEM((tm, D), BF16)],
        ),
        compiler_params=_cparams(("parallel", "arbitrary")),
        name="moe_experts",
    )(tile_e, used, xs, w_gu, w_gu, w_down)


def _combine_kernel(lo_ref, hi_ref, x_ref, gates_ref, ys_ref, g_ref, b_ref, p_ref, proj_ref, gw_ref, o_ref,
                    ya_ref, yb_ref, sems, *, tm, alpha):
    step = pl.program_id(0)
    slot = step % 2

    def gather(tile, dst_slot):
        def start(r, c):
            _row_copy(ys_ref, lo_ref[tile * tm + r], ya_ref.at[dst_slot], r, sems.at[dst_slot]).start()
            _row_copy(ys_ref, hi_ref[tile * tm + r], yb_ref.at[dst_slot], r, sems.at[dst_slot]).start()
            return c
        lax.fori_loop(0, tm, start, 0)

    @pl.when(step == 0)
    def _():
        gather(step, slot)

    @pl.when(step + 1 < pl.num_programs(0))
    def _():
        gather(step + 1, 1 - slot)

    gates = gates_ref[...]
    lane = lax.broadcasted_iota(jnp.int32, gates.shape, 1)
    sel = gates > 0.0
    e_lo = jnp.min(jnp.where(sel, lane, LANES), axis=-1, keepdims=True)
    e_hi = jnp.max(jnp.where(sel, lane, -1), axis=-1, keepdims=True)
    g_lo = jnp.sum(jnp.where(lane == e_lo, gates, 0.0), axis=-1, keepdims=True)
    g_hi = jnp.where(e_hi != e_lo, jnp.sum(jnp.where(lane == e_hi, gates, 0.0), axis=-1, keepdims=True), 0.0)
    pltpu.make_async_copy(ys_ref.at[pl.ds(0, tm)], ya_ref.at[slot], sems.at[slot]).wait()
    pltpu.make_async_copy(ys_ref.at[pl.ds(0, tm)], yb_ref.at[slot], sems.at[slot]).wait()
    ff = g_lo * ya_ref[slot] + g_hi * yb_ref[slot]
    o_ref[...] = _ln_ple(x_ref[...], ff, g_ref, b_ref, p_ref, proj_ref, gw_ref, alpha)


def _combine(pos_lo, pos_hi, x, gates, ys, g, b, p, proj, gate_w, alpha):
    N, D = x.shape
    P = p.shape[1]
    tm = _pick_tile(N, 256)
    row = lambda i, lo, hi: (i, 0)
    const = lambda i, lo, hi: (0, 0)
    return pl.pallas_call(
        functools.partial(_combine_kernel, tm=tm, alpha=alpha),
        out_shape=jax.ShapeDtypeStruct((N, D), F32),
        grid_spec=pltpu.PrefetchScalarGridSpec(
            num_scalar_prefetch=2,
            grid=(N // tm,),
            in_specs=[pl.BlockSpec((tm, D), row),
                      pl.BlockSpec((tm, LANES), row),
                      pl.BlockSpec(memory_space=pl.ANY),
                      pl.BlockSpec((1, D), const),
                      pl.BlockSpec((1, D), const),
                      pl.BlockSpec((tm, P), row),
                      pl.BlockSpec((P, D), const),
                      pl.BlockSpec((D, D), const)],
            out_specs=pl.BlockSpec((tm, D), row),
            scratch_shapes=[pltpu.VMEM((2, tm, D), F32), pltpu.VMEM((2, tm, D), F32),
                            pltpu.SemaphoreType.DMA((2,))],
        ),
        compiler_params=_cparams(("arbitrary",)),
        name="moe_combine",
    )(pos_lo, pos_hi, x, gates, ys, g, b, p, proj, gate_w)


def _moe(x, wr_pad, n_e, w_gu, w_down, g, b, p, proj, gate_w, alpha):
    gates = _router(x, wr_pad, n_e)
    tm = MOE_ROW_TILE
    pos_lo, pos_hi, tile_e, used, rows = _route_plan(gates, n_e, tm)
    xs = _dispatch(pos_lo, pos_hi, x, rows)
    ys = _experts(tile_e, used, xs, w_gu, w_down, tm)
    return _combine(pos_lo, pos_hi, x, gates, ys, g, b, p, proj, gate_w, alpha)


def _kv_kernel(x_ref, g_ref, b_ref, w_ref, wvt_ref, kp_ref, vp_ref, ks_ref, vs_ref, kb_ref, vb_ref, vt_ref,
               *, n_main):
    xn = _layer_norm(x_ref[...], g_ref[...], b_ref[...]).astype(BF16)
    qk = kp_ref.shape[-1]
    k = _dot(xn, w_ref[:, :qk])
    v = _dot(xn, w_ref[:, qk:])

    @pl.when(pl.program_id(0) < n_main)
    def _():
        kp_ref[...] = k
        vp_ref[...] = v

    @pl.when(pl.program_id(0) >= n_main)
    def _():
        ks_ref[...] = k
        vs_ref[...] = v

    kb_ref[...] = k.astype(BF16)
    vb_ref[...] = v.astype(BF16)
    vt_ref[...] = _dot_nt(wvt_ref[...], xn).astype(BF16)


def _shared_kv(x, n_p, g, b, w_kv_bf, w_vt_bf, qk_width):
    N, D = x.shape
    n_s = N - n_p
    W = w_kv_bf.shape[1]
    vw = W - qk_width
    tm = _pick_tile(math.gcd(n_p, n_s), 512)
    n_main = n_p // tm
    row = lambda i: (i, 0)
    const = lambda i: (0, 0)
    head = lambda i: (jnp.minimum(i, n_main - 1), 0)
    tail = lambda i: (jnp.maximum(i - n_main, 0), 0)
    return pl.pallas_call(
        functools.partial(_kv_kernel, n_main=n_main),
        out_shape=(jax.ShapeDtypeStruct((n_p, qk_width), F32), jax.ShapeDtypeStruct((n_p, vw), F32),
                   jax.ShapeDtypeStruct((n_s, qk_width), F32), jax.ShapeDtypeStruct((n_s, vw), F32),
                   jax.ShapeDtypeStruct((N, qk_width), BF16), jax.ShapeDtypeStruct((N, vw), BF16),
                   jax.ShapeDtypeStruct((vw, N), BF16)),
        grid=(N // tm,),
        in_specs=[pl.BlockSpec((tm, D), row), pl.BlockSpec((1, D), const), pl.BlockSpec((1, D), const),
                  pl.BlockSpec((D, W), const), pl.BlockSpec((vw, D), const)],
        out_specs=(pl.BlockSpec((tm, qk_width), head), pl.BlockSpec((tm, vw), head),
                   pl.BlockSpec((tm, qk_width), tail), pl.BlockSpec((tm, vw), tail),
                   pl.BlockSpec((tm, qk_width), row), pl.BlockSpec((tm, vw), row),
                   pl.BlockSpec((vw, tm), lambda i: (0, i))),
        compiler_params=_cparams(("arbitrary",)),
        name="shared_kv",
    )(x, g, b, w_kv_bf, w_vt_bf)


def _proj_kernel(x_ref, w_ref, o_ref, *, scale):
    o_ref[...] = (_dot(x_ref[...].astype(BF16), w_ref[...]) * scale).astype(o_ref.dtype)


def _proj(x, w_bf, scale):
    N, D = x.shape
    W = w_bf.shape[1]
    tm = _pick_tile(N, 512)
    return pl.pallas_call(
        functools.partial(_proj_kernel, scale=scale),
        out_shape=jax.ShapeDtypeStruct((N, W), BF16),
        grid=(N // tm,),
        in_specs=[pl.BlockSpec((tm, D), lambda i: (i, 0)),
                  pl.BlockSpec((D, W), lambda i: (0, 0))],
        out_specs=pl.BlockSpec((tm, W), lambda i: (i, 0)),
        compiler_params=_cparams(("parallel",)),
        name="q_proj",
    )(x, w_bf)


def _out_ln_kernel(op_ref, os_ref, w_ref, x_ref, g_ref, b_ref, y_ref, *, n_main, alpha):
    o = jnp.where(pl.program_id(0) < n_main, op_ref[...], os_ref[...])
    y_ref[...] = _layer_norm(alpha * x_ref[...] + _dot(o, w_ref[...]), g_ref[...], b_ref[...])


def _out_ln(o_p, o_s, w_bf, x, g, b, alpha):
    N, D = x.shape
    W = o_p.shape[1]
    tm = _pick_tile(math.gcd(o_p.shape[0], o_s.shape[0]), 512)
    n_main = o_p.shape[0] // tm
    return pl.pallas_call(
        functools.partial(_out_ln_kernel, n_main=n_main, alpha=alpha),
        out_shape=jax.ShapeDtypeStruct((N, D), F32),
        grid=(N // tm,),
        in_specs=[pl.BlockSpec((tm, W), lambda i: (jnp.minimum(i, n_main - 1), 0)),
                  pl.BlockSpec((tm, W), lambda i: (jnp.maximum(i - n_main, 0), 0)),
                  pl.BlockSpec((W, D), lambda i: (0, 0)),
                  pl.BlockSpec((tm, D), lambda i: (i, 0)),
                  pl.BlockSpec((1, D), lambda i: (0, 0)),
                  pl.BlockSpec((1, D), lambda i: (0, 0))],
        out_specs=pl.BlockSpec((tm, D), lambda i: (i, 0)),
        compiler_params=_cparams(("parallel",)),
        name="attn_out_ln",
    )(o_p, o_s, w_bf, x, g, b)


def _lambda_value(lam_ref, lam_init):
    lq = lam_ref[...]
    s1 = jnp.sum(lq[0:1] * lq[1:2], axis=-1, keepdims=True)
    s2 = jnp.sum(lq[2:3] * lq[3:4], axis=-1, keepdims=True)
    return jnp.exp(s1) - jnp.exp(s2) + lam_init


def _head_norm(o, g, lam_init):
    return o * lax.rsqrt(jnp.mean(o * o, axis=-1, keepdims=True) + RMS_EPS) * g * (1.0 - lam_init)


def _softmax_update(s, v, m_ref, l_ref, acc_ref, idx):
    m_old = m_ref[idx]
    m_new = jnp.maximum(m_old, jnp.max(s, axis=-1, keepdims=True))
    a = jnp.exp(m_old - m_new)
    p = jnp.exp(s - m_new)
    l_ref[idx] = a * l_ref[idx] + jnp.sum(p, axis=-1, keepdims=True)
    acc_ref[idx] = a * acc_ref[idx] + _dot(p.astype(BF16), v)
    m_ref[idx] = m_new


KPOS_RADIX = 256
PROMPT_HEADS_PER_STEP = 4
STAT_ROWS = 8
SUM_ROWS = 16


def _attn_prompt_kernel(ti_ref, tj_ref, slopes_ref, lam_ref, q_ref, k_ref, vt_ref, g_ref, o_ref,
                        qa_ref, m_ref, acc_ref, *, tq, hd, hp, lam_init):
    hg, pair = pl.program_id(1), pl.program_id(2)
    i, j = ti_ref[pair], tj_ref[pair]
    vd = 2 * hd

    @pl.when(j == 0)
    def _():
        m_ref[...] = jnp.full_like(m_ref, NEG_INF)
        acc_ref[...] = jnp.zeros_like(acc_ref)
        lane = lax.broadcasted_iota(jnp.int32, (tq, vd), 1)
        for hh in range(hp):
            q = q_ref[:, hh * vd:(hh + 1) * vd]
            slope = slopes_ref[hg * hp + hh]
            aug = jnp.where(lane == 0, slope * KPOS_RADIX, jnp.where(lane == 1, slope, 0.0)).astype(BF16)
            qa_ref[hh, 0:tq, 0:vd] = jnp.where(lane < hd, q, jnp.zeros_like(q))
            qa_ref[hh, tq:, 0:vd] = jnp.where(lane >= hd, q, jnp.zeros_like(q))
            qa_ref[hh, 0:tq, vd:] = aug
            qa_ref[hh, tq:, vd:] = aug

    def block(diagonal):
        kpos = lax.broadcasted_iota(jnp.int32, (tq, vd), 0) + j * tq
        lane = lax.broadcasted_iota(jnp.int32, (tq, vd), 1)
        kaug = jnp.where(lane == 0, kpos // KPOS_RADIX, jnp.where(lane == 1, kpos % KPOS_RADIX, 0))
        kaug = kaug.astype(F32).astype(BF16)
        ones = jnp.ones((SUM_ROWS, tq), BF16)
        if diagonal:
            kr = lax.broadcasted_iota(jnp.int32, (tq, 2 * tq), 0)
            qc = lax.broadcasted_iota(jnp.int32, (tq, 2 * tq), 1)
            visible = jnp.where(qc >= tq, qc - tq, qc) >= kr

        def scores(hh):
            ka = jnp.concatenate([k_ref[:, hh * vd:(hh + 1) * vd], kaug], axis=1)
            return _dot_nt(ka, qa_ref[hh])

        s_next = scores(0)
        for hh in range(hp):
            s = s_next
            if hh + 1 < hp:
                s_next = scores(hh + 1)
            if diagonal:
                s = jnp.where(visible, s, NEG_INF)
            m_old = m_ref[hh]
            m_new = jnp.maximum(m_old, jnp.max(s, axis=0, keepdims=True))
            a = jnp.exp(m_old[0:1] - m_new[0:1])
            p = jnp.exp(s - m_new[0:1]).astype(BF16)
            vt1 = jnp.concatenate([vt_ref[hh * vd:(hh + 1) * vd, :], ones], axis=0)
            acc_ref[hh] = a * acc_ref[hh] + _dot(vt1, p)
            m_ref[hh] = m_new

    @pl.when(j < i)
    def _():
        block(False)

    @pl.when(j == i)
    def _():
        block(True)
        lam = _lambda_value(lam_ref, lam_init)
        for hh in range(hp):
            acc = acc_ref[hh]
            ot = acc[:vd] / acc[vd:vd + 1]
            ot = ot[:, :tq] - lam * ot[:, tq:]
            ot = ot * lax.rsqrt(jnp.mean(ot * ot, axis=0, keepdims=True) + RMS_EPS)
            o = ot.T * g_ref[:, hh * vd:(hh + 1) * vd] * (1.0 - lam_init)
            o_ref[:, hh * vd:(hh + 1) * vd] = o.astype(o_ref.dtype)


def _attn_prompt(q, k, vt, slopes, lam_qk, subln_g, B, T, H, lam_init, hp):
    W = q.shape[1]
    vd = W // H
    hd = vd // 2
    tq = _pick_tile(T, 512)
    nq = T // tq
    assert T <= KPOS_RADIX * KPOS_RADIX and H % hp == 0
    pairs = [(i, j) for i in range(nq) for j in range(i + 1)]
    ti = jnp.asarray([p[0] for p in pairs], jnp.int32)
    tj = jnp.asarray([p[1] for p in pairs], jnp.int32)
    return pl.pallas_call(
        functools.partial(_attn_prompt_kernel, tq=tq, hd=hd, hp=hp, lam_init=lam_init),
        out_shape=jax.ShapeDtypeStruct((B * T, W), BF16),
        grid_spec=pltpu.PrefetchScalarGridSpec(
            num_scalar_prefetch=2,
            grid=(B, H // hp, len(pairs)),
            in_specs=[
                pl.BlockSpec(memory_space=pltpu.SMEM),
                pl.BlockSpec(lam_qk.shape, lambda b, h, p, ti, tj: (0, 0)),
                pl.BlockSpec((tq, hp * vd), lambda b, h, p, ti, tj: (b * nq + ti[p], h)),
                pl.BlockSpec((tq, hp * vd), lambda b, h, p, ti, tj: (b * nq + tj[p], h)),
                pl.BlockSpec((hp * vd, tq), lambda b, h, p, ti, tj: (h, b * nq + tj[p])),
                pl.BlockSpec((1, hp * vd), lambda b, h, p, ti, tj: (0, h)),
            ],
            out_specs=pl.BlockSpec((tq, hp * vd), lambda b, h, p, ti, tj: (b * nq + ti[p], h)),
            scratch_shapes=[pltpu.VMEM((hp, 2 * tq, 2 * vd), BF16), pltpu.VMEM((hp, STAT_ROWS, 2 * tq), F32),
                            pltpu.VMEM((hp, vd + SUM_ROWS, 2 * tq), F32)],
        ),
        compiler_params=_cparams(("parallel", "parallel", "arbitrary")),
        name="attn_prompt",
    )(ti, tj, slopes, lam_qk, q, k, vt, subln_g)


SAMPLE_PAGES_PER_STEP = 8


def _attn_sample_kernel(pt_ref, lam_ref, qz_ref, *refs, gp, n_steps, lam_init):
    k_refs, v_refs = refs[:gp], refs[gp:2 * gp]
    kn_ref, vn_ref, bias_ref, biasn_ref, g_ref, o_ref, m_ref, l_ref, acc_ref = refs[2 * gp:]
    st = pl.program_id(1)
    rows = k_refs[0].shape[1]

    @pl.when(st == 0)
    def _():
        m_ref[...] = jnp.full_like(m_ref, NEG_INF)
        l_ref[...] = jnp.zeros_like(l_ref)
        acc_ref[...] = jnp.zeros_like(acc_ref)

    qz = qz_ref[0]
    for step in range(n_steps):
        @pl.when(st == step)
        def _():
            ss = []
            for g in range(gp):
                pg = step * gp + g
                ss.append(_dot_nt(qz, k_refs[g][0].astype(BF16)) + bias_ref[:, pg * rows:(pg + 1) * rows])
            vb = jnp.concatenate([v_refs[g][0].astype(BF16) for g in range(gp)], axis=0)
            _softmax_update(jnp.concatenate(ss, axis=1), vb, m_ref, l_ref, acc_ref, 0)

    @pl.when(st == n_steps - 1)
    def _():
        _softmax_update(_dot_nt(qz, kn_ref[0]) + biasn_ref[...], vn_ref[0], m_ref, l_ref, acc_ref, 0)
        lam = _lambda_value(lam_ref, lam_init)
        o = acc_ref[0] / l_ref[0]
        half = o.shape[0] // 2
        o = o[:half] - lam * o[half:]
        o_ref[0] = _head_norm(o, g_ref[...], lam_init).astype(o_ref.dtype)


def _attn_sample(page_table, qz, cache_k2, cache_v2, k_new, v_new, bias, bias_new, g_rows, lam_qk, lam_init):
    Bs, R, vd = qz.shape
    n_pages = page_table.shape[1]
    gp = math.gcd(n_pages, SAMPLE_PAGES_PER_STEP)
    n_steps = n_pages // gp
    rows = cache_k2.shape[1]
    rn = k_new.shape[1]
    pt = page_table.reshape(-1)

    def page_spec(g):
        return pl.BlockSpec((1, rows, vd), lambda b, s, pt_ref: (pt_ref[b * n_pages + s * gp + g], 0, 0))

    const2 = lambda b, s, pt_ref: (0, 0)
    seq3 = lambda b, s, pt_ref: (b, 0, 0)
    return pl.pallas_call(
        functools.partial(_attn_sample_kernel, gp=gp, n_steps=n_steps, lam_init=lam_init),
        out_shape=jax.ShapeDtypeStruct((Bs, R // 2, vd), BF16),
        grid_spec=pltpu.PrefetchScalarGridSpec(
            num_scalar_prefetch=1,
            grid=(Bs, n_steps),
            in_specs=[pl.BlockSpec(lam_qk.shape, const2), pl.BlockSpec((1, R, vd), seq3)]
            + [page_spec(g) for g in range(gp)] + [page_spec(g) for g in range(gp)]
            + [pl.BlockSpec((1, rn, vd), seq3), pl.BlockSpec((1, rn, vd), seq3),
               pl.BlockSpec(bias.shape, const2), pl.BlockSpec(bias_new.shape, const2),
               pl.BlockSpec(g_rows.shape, const2)],
            out_specs=pl.BlockSpec((1, R // 2, vd), seq3),
            scratch_shapes=[pltpu.VMEM((1, R, 1), F32), pltpu.VMEM((1, R, 1), F32),
                            pltpu.VMEM((1, R, vd), F32)],
        ),
        compiler_params=_cparams(("parallel", "arbitrary")),
        name="attn_sample",
    )(pt, lam_qk, qz, *([cache_k2] * gp), *([cache_v2] * gp), k_new, v_new, bias, bias_new, g_rows)


def _alibi_slopes(H):
    return 2.0 ** (-8.0 * np.arange(1, H + 1, dtype=np.float64) / H)


def _sample_bias_tables(H, Ts, page, n_pages):
    slopes = _alibi_slopes(H)
    pos0 = n_pages * page
    hq = np.tile(np.repeat(np.arange(H), Ts), 2)
    tq = np.tile(np.arange(Ts), 2 * H)
    key = np.repeat(np.arange(n_pages * page), H)
    hk = np.tile(np.arange(H), n_pages * page)
    dist = (pos0 + tq)[:, None] - key[None, :]
    bias = np.where(hq[:, None] == hk[None, :], -slopes[hq][:, None] * dist, NEG_INF)
    tn = np.repeat(np.arange(Ts), H)
    hn = np.tile(np.arange(H), Ts)
    dn = tq[:, None] - tn[None, :]
    bias_new = np.where((hq[:, None] == hn[None, :]) & (dn >= 0), -slopes[hq][:, None] * dn, NEG_INF)
    return jnp.asarray(bias, F32), jnp.asarray(bias_new, F32)


def kernel(x_prompt, x_sample, state_pool, cache_k, cache_v, page_table, p_prompt, p_sample, pool_w, pool_scale, ln1_g, ln1_b, ln2_g, ln2_b, ln_kv_g, ln_kv_b, w_kv, w_q, lam_qk, subln_g, w_o, dense_gu, dense_down, w_router, moe_gu, moe_down, ple_proj, ple_gate):
    B, T, D = x_prompt.shape
    Bs, Ts, _ = x_sample.shape
    depth = ln1_g.shape[0]
    n_a = pool_w.shape[0]
    n_pool, page, H, vd = cache_v.shape
    hd = vd // 2
    qk_width = H * 2 * hd
    n_e = w_router.shape[-1]
    n_pages = page_table.shape[1]
    pos0 = n_pages * page
    n_p, n_s = B * T, Bs * Ts
    n_all = n_p + n_s
    alpha = (2 * depth) ** 0.25
    row = lambda a: a.reshape(1, -1)

    bf = lambda w: w.astype(BF16)
    wr_pad = jnp.pad(w_router, ((0, 0), (0, 0), (0, LANES - n_e)))

    p_all = jnp.concatenate([p_prompt.reshape(depth, n_p, -1), p_sample.reshape(depth, n_s, -1)], axis=1)
    slopes_np = _alibi_slopes(H)
    assert np.all((slopes_np * KPOS_RADIX).astype(BF16).astype(np.float64) == slopes_np * KPOS_RADIX)
    assert np.all(slopes_np.astype(BF16).astype(np.float64) == slopes_np)
    slopes = jnp.asarray(slopes_np, F32)
    bias, bias_new = _sample_bias_tables(H, Ts, page, n_pages)

    x_rows = x_prompt.reshape(n_p, D)
    x_s = x_sample
    x_all = None
    pool_p, pool_s = [], []
    kv = None
    for i in range(depth):
        if i < n_a:
            pool_p.append(x_rows[:n_p].reshape(B, T, D)[:, T - POOL_BUF:])
            pool_s.append(jnp.concatenate([state_pool[i][:, Ts:], x_s], axis=1))
            pw = bf(pool_w[i])
            x1_s = _pool_sample(state_pool[i].transpose(1, 0, 2), x_s.transpose(1, 0, 2), pw,
                                row(pool_scale[i]), row(ln1_g[i]), row(ln1_b[i]), alpha, pos0)
            x1 = _pool_prompt(x_rows, B, T, x1_s.transpose(1, 0, 2).reshape(n_s, D), pw, row(pool_scale[i]),
                              row(ln1_g[i]), row(ln1_b[i]), alpha)
        else:
            jj = i - n_a
            lam_init = 0.8 - 0.6 * math.exp(-0.3 * i)
            if kv is None:
                kv = _shared_kv(x_all, n_p, row(ln_kv_g), row(ln_kv_b), bf(w_kv), bf(w_kv[:, qk_width:].T),
                                qk_width)
            k_b, v_b, v_t = kv[4:]
            q = _proj(x_all, bf(w_q[jj]), hd ** -0.5)
            o = _attn_prompt(q, k_b, v_t, slopes, lam_qk[jj], row(subln_g[jj]), B, T, H, lam_init,
                             math.gcd(H, PROMPT_HEADS_PER_STEP))
            q_s = q[n_p:].reshape(Bs, Ts, H, vd).transpose(0, 2, 1, 3).reshape(Bs, H * Ts, vd)
            lane = jnp.arange(vd)
            qz = jnp.concatenate([jnp.where(lane < hd, q_s, 0), jnp.where(lane >= hd, q_s, 0)], axis=1)
            g_rows = jnp.repeat(subln_g[jj].reshape(H, vd), Ts, axis=0)
            o_s = _attn_sample(page_table, qz, cache_k.reshape(n_pool, page * H, 2 * hd),
                               cache_v.reshape(n_pool, page * H, vd),
                               k_b[n_p:].reshape(Bs, Ts * H, 2 * hd), v_b[n_p:].reshape(Bs, Ts * H, vd),
                               bias, bias_new, g_rows, lam_qk[jj], lam_init)
            o_s = o_s.reshape(Bs, H, Ts, vd).transpose(0, 2, 1, 3).reshape(n_s, H * vd)
            x1 = _out_ln(o, o_s, bf(w_o[jj]), x_all, row(ln1_g[i]), row(ln1_b[i]), alpha)
        tail_args = (row(ln2_g[i]), row(ln2_b[i]), p_all[i], bf(ple_proj[i]), bf(ple_gate[i]), alpha)
        if i % 2 == 0:
            x_all = _ffn(x1, bf(dense_gu[i // 2]), bf(dense_down[i // 2]), *tail_args)
        else:
            x_all = _moe(x1, wr_pad[i // 2], n_e, bf(moe_gu[i // 2]), bf(moe_down[i // 2]), *tail_args)
        x_rows = x_all
        x_s = x_all[n_p:].reshape(Bs, Ts, D)

    k_p, v_p, k_s, v_s = kv[:4]
    return (x_all[:n_p].reshape(B, T, D), x_s, jnp.stack(pool_p), jnp.stack(pool_s),
            k_p.reshape(B, T, H, 2 * hd), v_p.reshape(B, T, H, vd),
            k_s.reshape(Bs, Ts, H, 2 * hd), v_s.reshape(Bs, Ts, H, vd))
```

```python
import functools
import math

import numpy as np
import jax
import jax.numpy as jnp
from jax import lax
from jax.experimental import pallas as pl
from jax.experimental.pallas import tpu as pltpu

F32 = jnp.float32
BF16 = jnp.bfloat16

POOL_WINDOWS = (2, 4, 8, 16)
POOL_BUF = max(POOL_WINDOWS) - 1
POOL_HALO = 16
TOP_K = 2
LN_EPS = 1e-5
RMS_EPS = 1e-5
NEG_INF = -1e30
LANES = 128
VMEM_LIMIT = 56 * 1024 * 1024


def _cparams(sem):
    return pltpu.CompilerParams(dimension_semantics=sem, vmem_limit_bytes=VMEM_LIMIT)


def _layer_norm(x, g, b):
    mu = jnp.mean(x, axis=-1, keepdims=True)
    xc = x - mu
    var = jnp.mean(xc * xc, axis=-1, keepdims=True)
    return xc * lax.rsqrt(var + LN_EPS) * g + b


def _sigmoid(x):
    return 1.0 / (1.0 + jnp.exp(-x))


def _dot(a, b):
    return jnp.dot(a, b, preferred_element_type=F32)


def _dot_nt(a, b):
    return lax.dot_general(a, b, (((1,), (1,)), ((), ())), preferred_element_type=F32)


def _pick_tile(n, target):
    t = min(n, target)
    while n % t:
        t //= 2
    return t


def _pool_prompt_kernel(x_ref, halo_ref, tail_ref, w_ref, scale_ref, g_ref, b_ref, o_ref, ext_ref,
                        *, tm, nt, n_main, alpha):
    step = pl.program_id(0)

    @pl.when(step < n_main)
    def _():
        i = step % nt
        x = x_ref[...]
        ext_ref[0:POOL_HALO, :] = jnp.where(i > 0, halo_ref[...], 0.0)
        ext_ref[POOL_HALO:, :] = x
        gw = x.shape[-1] // len(POOL_WINDOWS)
        pos = lax.broadcasted_iota(jnp.int32, (tm, 1), 0) + i * tm
        ys = []
        for g, w in enumerate(POOL_WINDOWS):
            lo, hi = g * gw, (g + 1) * gw
            xs = x[:, lo:hi]
            s = xs
            for k in range(1, w):
                s = s + ext_ref[POOL_HALO - k:POOL_HALO - k + tm, lo:hi]
            cnt = jnp.minimum(w, pos + 1).astype(F32)
            d = s / cnt - xs
            ys.append(_dot(d.astype(BF16), w_ref[g]))
        y = jnp.concatenate(ys, axis=-1) * scale_ref[...]
        o_ref[...] = _layer_norm(alpha * x + y, g_ref[...], b_ref[...])

    @pl.when(step >= n_main)
    def _():
        o_ref[...] = tail_ref[...]


def _pool_prompt(x, B, T, tail, w_bf, scale, g, b, alpha):
    D = x.shape[1]
    n_tail = tail.shape[0]
    tm = _pick_tile(math.gcd(T, n_tail), 512)
    nt = T // tm
    n_main = B * nt
    hb = tm // POOL_HALO
    G, gw = w_bf.shape[0], w_bf.shape[1]
    main = lambda s: jnp.minimum(s, n_main - 1)
    const2 = lambda s: (0, 0)
    return pl.pallas_call(
        functools.partial(_pool_prompt_kernel, tm=tm, nt=nt, n_main=n_main, alpha=alpha),
        out_shape=jax.ShapeDtypeStruct((B * T + n_tail, D), F32),
        grid=(n_main + n_tail // tm,),
        in_specs=[
            pl.BlockSpec((tm, D), lambda s: (main(s), 0)),
            pl.BlockSpec((POOL_HALO, D), lambda s: (jnp.maximum(main(s) * hb - 1, 0), 0)),
            pl.BlockSpec((tm, D), lambda s: (jnp.maximum(s - n_main, 0), 0)),
            pl.BlockSpec((G, gw, gw), lambda s: (0, 0, 0)),
            pl.BlockSpec((1, D), const2),
            pl.BlockSpec((1, D), const2),
            pl.BlockSpec((1, D), const2),
        ],
        out_specs=pl.BlockSpec((tm, D), lambda s: (s, 0)),
        scratch_shapes=[pltpu.VMEM((tm + POOL_HALO, D), F32)],
        compiler_params=_cparams(("arbitrary",)),
        name="pool_prompt",
    )(x, x, tail, w_bf, scale, g, b)


def _pool_sample_kernel(buf_ref, x_ref, w_ref, scale_ref, g_ref, b_ref, o_ref, *, alpha, pos0):
    nb, ts = buf_ref.shape[0], x_ref.shape[0]
    rows = [buf_ref[r] for r in range(nb)] + [x_ref[t] for t in range(ts)]
    gw = rows[0].shape[-1] // len(POOL_WINDOWS)
    for t in range(ts):
        x = rows[nb + t]
        ys = []
        for g, w in enumerate(POOL_WINDOWS):
            lo, hi = g * gw, (g + 1) * gw
            xs = x[:, lo:hi]
            s = xs
            for k in range(1, w):
                s = s + rows[nb + t - k][:, lo:hi]
            d = s / float(min(w, pos0 + t + 1)) - xs
            ys.append(_dot(d.astype(BF16), w_ref[g]))
        y = jnp.concatenate(ys, axis=-1) * scale_ref[...]
        o_ref[t] = _layer_norm(alpha * x + y, g_ref[...], b_ref[...])


def _pool_sample(buf_t, x_t, w_bf, scale, g, b, alpha, pos0):
    nb, Bs, D = buf_t.shape
    ts = x_t.shape[0]
    bs = _pick_tile(Bs, 32)
    G, gw = w_bf.shape[0], w_bf.shape[1]
    return pl.pallas_call(
        functools.partial(_pool_sample_kernel, alpha=alpha, pos0=pos0),
        out_shape=jax.ShapeDtypeStruct((ts, Bs, D), F32),
        grid=(Bs // bs,),
        in_specs=[
            pl.BlockSpec((nb, bs, D), lambda i: (0, i, 0)),
            pl.BlockSpec((ts, bs, D), lambda i: (0, i, 0)),
            pl.BlockSpec((G, gw, gw), lambda i: (0, 0, 0)),
            pl.BlockSpec((1, D), lambda i: (0, 0)),
            pl.BlockSpec((1, D), lambda i: (0, 0)),
            pl.BlockSpec((1, D), lambda i: (0, 0)),
        ],
        out_specs=pl.BlockSpec((ts, bs, D), lambda i: (0, i, 0)),
        compiler_params=_cparams(("parallel",)),
        name="pool_sample",
    )(buf_t, x_t, w_bf, scale, g, b)


def _ln_ple(x, ff, g_ref, b_ref, p_ref, proj_ref, gw_ref, alpha):
    x2 = _layer_norm(alpha * x + ff, g_ref[...], b_ref[...])
    pp = _dot(p_ref[...].astype(BF16), proj_ref[...])
    gt = _dot(x2.astype(BF16), gw_ref[...])
    return x2 + pp * _sigmoid(gt)


def _swiglu_partial(xb, wg, wu, wd):
    h = _dot(xb, wg)
    u = _dot(xb, wu)
    return _dot((h * _sigmoid(h) * u).astype(BF16), wd)


def _ffn_kernel(x_ref, wg_ref, wu_ref, wd_ref, g_ref, b_ref, p_ref, proj_ref, gw_ref, o_ref, acc_ref, xb_ref,
                *, alpha):
    j = pl.program_id(1)

    @pl.when(j == 0)
    def _():
        acc_ref[...] = jnp.zeros_like(acc_ref)
        xb_ref[...] = x_ref[...].astype(BF16)

    acc_ref[...] += _swiglu_partial(xb_ref[...], wg_ref[...], wu_ref[...], wd_ref[...])

    @pl.when(j == pl.num_programs(1) - 1)
    def _():
        o_ref[...] = _ln_ple(x_ref[...], acc_ref[...], g_ref, b_ref, p_ref, proj_ref, gw_ref, alpha)


def _ffn_tile(F, target):
    best = LANES
    for t in range(LANES, min(F, target) + 1, LANES):
        if F % t == 0:
            best = t
    return best


def _ffn(x, w_gu, w_down, g, b, p, proj, gate_w, alpha):
    N, D = x.shape
    F = w_down.shape[0]
    P = p.shape[1]
    tm = _pick_tile(N, 512)
    tf = _ffn_tile(F, 1536)
    nf = F // tf
    return pl.pallas_call(
        functools.partial(_ffn_kernel, alpha=alpha),
        out_shape=jax.ShapeDtypeStruct((N, D), F32),
        grid=(N // tm, nf),
        in_specs=[
            pl.BlockSpec((tm, D), lambda i, j: (i, 0)),
            pl.BlockSpec((D, tf), lambda i, j: (0, j)),
            pl.BlockSpec((D, tf), lambda i, j: (0, nf + j)),
            pl.BlockSpec((tf, D), lambda i, j: (j, 0)),
            pl.BlockSpec((1, D), lambda i, j: (0, 0)),
            pl.BlockSpec((1, D), lambda i, j: (0, 0)),
            pl.BlockSpec((tm, P), lambda i, j: (i, 0)),
            pl.BlockSpec((P, D), lambda i, j: (0, 0)),
            pl.BlockSpec((D, D), lambda i, j: (0, 0)),
        ],
        out_specs=pl.BlockSpec((tm, D), lambda i, j: (i, 0)),
        scratch_shapes=[pltpu.VMEM((tm, D), F32), pltpu.VMEM((tm, D), BF16)],
        compiler_params=_cparams(("parallel", "arbitrary")),
        name="ffn_dense",
    )(x, w_gu, w_gu, w_down, g, b, p, proj, gate_w)


def _router_kernel(x_ref, wr_ref, o_ref, *, n_e):
    lg = jnp.dot(x_ref[...], wr_ref[...], preferred_element_type=F32, precision=lax.Precision.HIGHEST)
    lane = lax.broadcasted_iota(jnp.int32, lg.shape, 1)
    lg = jnp.where(lane < n_e, lg, -jnp.inf)
    v1 = jnp.max(lg, axis=-1, keepdims=True)
    i1 = jnp.min(jnp.where(lg == v1, lane, LANES), axis=-1, keepdims=True)
    lg2 = jnp.where(lane == i1, -jnp.inf, lg)
    v2 = jnp.max(lg2, axis=-1, keepdims=True)
    i2 = jnp.min(jnp.where(lg2 == v2, lane, LANES), axis=-1, keepdims=True)
    e2 = jnp.exp(v2 - v1)
    den = 1.0 + e2
    o_ref[...] = jnp.where(lane == i1, 1.0 / den, 0.0) + jnp.where(lane == i2, e2 / den, 0.0)


def _router(x, wr_pad, n_e):
    N, D = x.shape
    tm = _pick_tile(N, 512)
    return pl.pallas_call(
        functools.partial(_router_kernel, n_e=n_e),
        out_shape=jax.ShapeDtypeStruct((N, LANES), F32),
        grid=(N // tm,),
        in_specs=[pl.BlockSpec((tm, D), lambda i: (i, 0)),
                  pl.BlockSpec((D, LANES), lambda i: (0, 0))],
        out_specs=pl.BlockSpec((tm, LANES), lambda i: (i, 0)),
        compiler_params=_cparams(("parallel",)),
        name="router",
    )(x, wr_pad)


MOE_ROW_TILE = 1024
MOE_SUB_TILE = 512


def _route_plan(gates, n_e, tm):
    N = gates.shape[0]
    n_tiles = -(-TOP_K * N // tm) + n_e
    sel = gates[:, :n_e] > 0.0
    cnt = jnp.sum(sel, axis=0, dtype=jnp.int32)
    padded = (cnt + tm - 1) // tm * tm
    ends = jnp.cumsum(padded)
    rank = jnp.cumsum(sel.astype(jnp.int32), axis=0) - 1
    pos = (ends - padded)[None, :] + rank
    rows = n_tiles * tm
    pos_lo = jnp.minimum(jnp.min(jnp.where(sel, pos, rows), axis=1), rows - 1)
    pos_hi = jnp.max(jnp.where(sel, pos, -1), axis=1)
    pos_hi = jnp.where(pos_hi < 0, pos_lo, pos_hi)
    tile_start = jnp.arange(n_tiles, dtype=jnp.int32) * tm
    tile_e = jnp.minimum(jnp.sum(tile_start[:, None] >= ends[None, :], axis=1), n_e - 1).astype(jnp.int32)
    used = jnp.clip(jnp.take(ends - padded + cnt, tile_e) - tile_start, 0, tm)
    used = jnp.where(tile_start < ends[-1], used, 0).astype(jnp.int32)
    last_e = jnp.max(jnp.where(used > 0, tile_e, 0))
    tile_e = jnp.where(tile_start < ends[-1], tile_e, last_e)
    return pos_lo.astype(jnp.int32), pos_hi.astype(jnp.int32), tile_e, used, rows


def _row_copy(src_ref, src_row, dst_ref, dst_row, sem):
    return pltpu.make_async_copy(src_ref.at[pl.ds(src_row, 1)], dst_ref.at[pl.ds(dst_row, 1)], sem)


def _dispatch_kernel(lo_ref, hi_ref, x_ref, xs_in_ref, xs_ref, sem, *, tm):
    del xs_in_ref
    base = pl.program_id(0) * tm

    def start(r, c):
        _row_copy(x_ref, r, xs_ref, lo_ref[base + r], sem).start()
        _row_copy(x_ref, r, xs_ref, hi_ref[base + r], sem).start()
        return c

    lax.fori_loop(0, tm, start, 0)
    for _ in range(TOP_K):
        pltpu.make_async_copy(x_ref, xs_ref.at[pl.ds(0, tm)], sem).wait()


def _dispatch(pos_lo, pos_hi, x, rows):
    N, D = x.shape
    tm = _pick_tile(N, 256)
    return pl.pallas_call(
        functools.partial(_dispatch_kernel, tm=tm),
        out_shape=jax.ShapeDtypeStruct((rows, D), F32),
        grid_spec=pltpu.PrefetchScalarGridSpec(
            num_scalar_prefetch=2,
            grid=(N // tm,),
            in_specs=[pl.BlockSpec((tm, D), lambda i, lo, hi: (i, 0)),
                      pl.BlockSpec(memory_space=pl.ANY)],
            out_specs=pl.BlockSpec(memory_space=pl.ANY),
            scratch_shapes=[pltpu.SemaphoreType.DMA],
        ),
        input_output_aliases={3: 0},
        compiler_params=_cparams(("arbitrary",)),
        name="moe_dispatch",
    )(pos_lo, pos_hi, x, jnp.zeros((rows, D), F32))


def _experts_kernel(te_ref, used_ref, x_ref, wg_ref, wu_ref, wd_ref, o_ref, acc_ref, xb_ref, *, sub):
    r, j = pl.program_id(0), pl.program_id(1)
    used = used_ref[r]
    tm = x_ref.shape[0]
    n_sub = tm // sub

    @pl.when(j == 0)
    def _():
        acc_ref[...] = jnp.zeros_like(acc_ref)
        xb_ref[...] = x_ref[...].astype(BF16)

    @pl.when(used > tm - sub)
    def _():
        acc_ref[...] += _swiglu_partial(xb_ref[...], wg_ref[...], wu_ref[...], wd_ref[...])

    for s in range(n_sub - 1):
        @pl.when((used > s * sub) & (used <= tm - sub))
        def _():
            rs = pl.ds(s * sub, sub)
            acc_ref[rs, :] += _swiglu_partial(xb_ref[rs, :], wg_ref[...], wu_ref[...], wd_ref[...])

    @pl.when(j == pl.num_programs(1) - 1)
    def _():
        o_ref[...] = acc_ref[...]


def _experts(tile_e, used, xs, w_gu, w_down, layer, tm):
    rows, D = xs.shape
    F = w_down.shape[2]
    tf = _ffn_tile(F, 512)
    nf = F // tf

    def jmap(r, j, used_ref):
        return jnp.where(used_ref[r] > 0, j, nf - 1)

    return pl.pallas_call(
        functools.partial(_experts_kernel, sub=min(MOE_SUB_TILE, tm)),
        out_shape=jax.ShapeDtypeStruct((rows, D), F32),
        grid_spec=pltpu.PrefetchScalarGridSpec(
            num_scalar_prefetch=2,
            grid=(rows // tm, nf),
            in_specs=[
                pl.BlockSpec((tm, D), lambda r, j, te, us: (r, 0)),
                pl.BlockSpec((None, None, D, tf), lambda r, j, te, us: (layer, te[r], 0, jmap(r, j, us))),
                pl.BlockSpec((None, None, D, tf), lambda r, j, te, us: (layer, te[r], 0, nf + jmap(r, j, us))),
                pl.BlockSpec((None, None, tf, D), lambda r, j, te, us: (layer, te[r], jmap(r, j, us), 0)),
            ],
            out_specs=pl.BlockSpec((tm, D), lambda r, j, te, us: (r, 0)),
            scratch_shapes=[pltpu.VMEM((tm, D), F32), pltpu.VMEM((tm, D), BF16)],
        ),
        compiler_params=_cparams(("parallel", "arbitrary")),
        name="moe_experts",
    )(tile_e, used, xs, w_gu, w_gu, w_down)


def _combine_kernel(lo_ref, hi_ref, x_ref, gates_ref, ys_ref, g_ref, b_ref, p_ref, proj_ref, gw_ref, o_ref,
                    ya_ref, yb_ref, sems, *, tm, alpha):
    step = pl.program_id(0)
    slot = step % 2

    last = pl.num_programs(0) - 1

    def start(tile, dst_slot, r):
        _row_copy(ys_ref, lo_ref[tile * tm + r], ya_ref.at[dst_slot], r, sems.at[dst_slot]).start()
        _row_copy(ys_ref, hi_ref[tile * tm + r], yb_ref.at[dst_slot], r, sems.at[dst_slot]).start()

    def wait(w_slot):
        pltpu.make_async_copy(ys_ref.at[pl.ds(0, tm)], ya_ref.at[w_slot], sems.at[w_slot]).wait()
        pltpu.make_async_copy(ys_ref.at[pl.ds(0, tm)], yb_ref.at[w_slot], sems.at[w_slot]).wait()

    @pl.when(step == 0)
    def _():
        lax.fori_loop(0, tm, lambda r, c: (start(step, slot, r), c)[1], 0)

    nxt = jnp.minimum(step + 1, last)
    for r in range(tm):
        start(nxt, 1 - slot, r)

    gates = gates_ref[...]
    lane = lax.broadcasted_iota(jnp.int32, gates.shape, 1)
    sel = gates > 0.0
    e_lo = jnp.min(jnp.where(sel, lane, LANES), axis=-1, keepdims=True)
    e_hi = jnp.max(jnp.where(sel, lane, -1), axis=-1, keepdims=True)
    g_lo = jnp.sum(jnp.where(lane == e_lo, gates, 0.0), axis=-1, keepdims=True)
    g_hi = jnp.where(e_hi != e_lo, jnp.sum(jnp.where(lane == e_hi, gates, 0.0), axis=-1, keepdims=True), 0.0)
    wait(slot)
    ff = g_lo * ya_ref[slot] + g_hi * yb_ref[slot]
    o_ref[...] = _ln_ple(x_ref[...], ff, g_ref, b_ref, p_ref, proj_ref, gw_ref, alpha)

    @pl.when(step == last)
    def _():
        wait(1 - slot)


def _combine(pos_lo, pos_hi, x, gates, ys, g, b, p, proj, gate_w, alpha):
    N, D = x.shape
    P = p.shape[1]
    tm = _pick_tile(N, 256)
    row = lambda i, lo, hi: (i, 0)
    const = lambda i, lo, hi: (0, 0)
    return pl.pallas_call(
        functools.partial(_combine_kernel, tm=tm, alpha=alpha),
        out_shape=jax.ShapeDtypeStruct((N, D), F32),
        grid_spec=pltpu.PrefetchScalarGridSpec(
            num_scalar_prefetch=2,
            grid=(N // tm,),
            in_specs=[pl.BlockSpec((tm, D), row),
                      pl.BlockSpec((tm, LANES), row),
                      pl.BlockSpec(memory_space=pl.ANY),
                      pl.BlockSpec((1, D), const),
                      pl.BlockSpec((1, D), const),
                      pl.BlockSpec((tm, P), row),
                      pl.BlockSpec((P, D), const),
                      pl.BlockSpec((D, D), const)],
            out_specs=pl.BlockSpec((tm, D), row),
            scratch_shapes=[pltpu.VMEM((2, tm, D), F32), pltpu.VMEM((2, tm, D), F32),
                            pltpu.SemaphoreType.DMA((2,))],
        ),
        compiler_params=_cparams(("arbitrary",)),
        name="moe_combine",
    )(pos_lo, pos_hi, x, gates, ys, g, b, p, proj, gate_w)


def _moe(x, wr_pad, n_e, w_gu, w_down, layer, g, b, p, proj, gate_w, alpha):
    gates = _router(x, wr_pad, n_e)
    tm = MOE_ROW_TILE
    pos_lo, pos_hi, tile_e, used, rows = _route_plan(gates, n_e, tm)
    xs = _dispatch(pos_lo, pos_hi, x, rows)
    ys = _experts(tile_e, used, xs, w_gu, w_down, layer, tm)
    return _combine(pos_lo, pos_hi, x, gates, ys, g, b, p, proj, gate_w, alpha)


def _kv_kernel(x_ref, g_ref, b_ref, w_ref, wvt_ref, kp_ref, vp_ref, ks_ref, vs_ref, kb_ref, vb_ref, vt_ref,
               *, n_main):
    xn = _layer_norm(x_ref[...], g_ref[...], b_ref[...]).astype(BF16)
    qk = kp_ref.shape[-1]
    k = _dot(xn, w_ref[:, :qk])
    v = _dot(xn, w_ref[:, qk:])

    @pl.when(pl.program_id(0) < n_main)
    def _():
        kp_ref[...] = k
        vp_ref[...] = v

    @pl.when(pl.program_id(0) >= n_main)
    def _():
        ks_ref[...] = k
        vs_ref[...] = v

    kb_ref[...] = k.astype(BF16)
    vb_ref[...] = v.astype(BF16)
    vt_ref[...] = _dot_nt(wvt_ref[...], xn).astype(BF16)


def _shared_kv(x, n_p, g, b, w_kv_bf, w_vt_bf, qk_width):
    N, D = x.shape
    n_s = N - n_p
    W = w_kv_bf.shape[1]
    vw = W - qk_width
    tm = _pick_tile(math.gcd(n_p, n_s), 512)
    n_main = n_p // tm
    row = lambda i: (i, 0)
    const = lambda i: (0, 0)
    head = lambda i: (jnp.minimum(i, n_main - 1), 0)
    tail = lambda i: (jnp.maximum(i - n_main, 0), 0)
    return pl.pallas_call(
        functools.partial(_kv_kernel, n_main=n_main),
        out_shape=(jax.ShapeDtypeStruct((n_p, qk_width), F32), jax.ShapeDtypeStruct((n_p, vw), F32),
                   jax.ShapeDtypeStruct((n_s, qk_width), F32), jax.ShapeDtypeStruct((n_s, vw), F32),
                   jax.ShapeDtypeStruct((N, qk_width), BF16), jax.ShapeDtypeStruct((N, vw), BF16),
                   jax.ShapeDtypeStruct((vw, N), BF16)),
        grid=(N // tm,),
        in_specs=[pl.BlockSpec((tm, D), row), pl.BlockSpec((1, D), const), pl.BlockSpec((1, D), const),
                  pl.BlockSpec((D, W), const), pl.BlockSpec((vw, D), const)],
        out_specs=(pl.BlockSpec((tm, qk_width), head), pl.BlockSpec((tm, vw), head),
                   pl.BlockSpec((tm, qk_width), tail), pl.BlockSpec((tm, vw), tail),
                   pl.BlockSpec((tm, qk_width), row), pl.BlockSpec((tm, vw), row),
                   pl.BlockSpec((vw, tm), lambda i: (0, i))),
        compiler_params=_cparams(("arbitrary",)),
        name="shared_kv",
    )(x, g, b, w_kv_bf, w_vt_bf)


def _proj_kernel(x_ref, w_ref, o_ref, *, scale):
    o_ref[...] = (_dot(x_ref[...].astype(BF16), w_ref[...]) * scale).astype(o_ref.dtype)


def _proj(x, w_bf, scale):
    N, D = x.shape
    W = w_bf.shape[1]
    tm = _pick_tile(N, 512)
    return pl.pallas_call(
        functools.partial(_proj_kernel, scale=scale),
        out_shape=jax.ShapeDtypeStruct((N, W), BF16),
        grid=(N // tm,),
        in_specs=[pl.BlockSpec((tm, D), lambda i: (i, 0)),
                  pl.BlockSpec((D, W), lambda i: (0, 0))],
        out_specs=pl.BlockSpec((tm, W), lambda i: (i, 0)),
        compiler_params=_cparams(("parallel",)),
        name="q_proj",
    )(x, w_bf)


def _out_ln_kernel(op_ref, os_ref, w_ref, x_ref, g_ref, b_ref, y_ref, *, n_main, alpha):
    o = jnp.where(pl.program_id(0) < n_main, op_ref[...], os_ref[...])
    y_ref[...] = _layer_norm(alpha * x_ref[...] + _dot(o, w_ref[...]), g_ref[...], b_ref[...])


def _out_ln(o_p, o_s, w_bf, x, g, b, alpha):
    N, D = x.shape
    W = o_p.shape[1]
    tm = _pick_tile(math.gcd(o_p.shape[0], o_s.shape[0]), 512)
    n_main = o_p.shape[0] // tm
    return pl.pallas_call(
        functools.partial(_out_ln_kernel, n_main=n_main, alpha=alpha),
        out_shape=jax.ShapeDtypeStruct((N, D), F32),
        grid=(N // tm,),
        in_specs=[pl.BlockSpec((tm, W), lambda i: (jnp.minimum(i, n_main - 1), 0)),
                  pl.BlockSpec((tm, W), lambda i: (jnp.maximum(i - n_main, 0), 0)),
                  pl.BlockSpec((W, D), lambda i: (0, 0)),
                  pl.BlockSpec((tm, D), lambda i: (i, 0)),
                  pl.BlockSpec((1, D), lambda i: (0, 0)),
                  pl.BlockSpec((1, D), lambda i: (0, 0))],
        out_specs=pl.BlockSpec((tm, D), lambda i: (i, 0)),
        compiler_params=_cparams(("parallel",)),
        name="attn_out_ln",
    )(o_p, o_s, w_bf, x, g, b)


def _lambda_value(lam_ref, lam_init):
    lq = lam_ref[...]
    s1 = jnp.sum(lq[0:1] * lq[1:2], axis=-1, keepdims=True)
    s2 = jnp.sum(lq[2:3] * lq[3:4], axis=-1, keepdims=True)
    return jnp.exp(s1) - jnp.exp(s2) + lam_init


def _head_norm(o, g, lam_init):
    return o * lax.rsqrt(jnp.mean(o * o, axis=-1, keepdims=True) + RMS_EPS) * g * (1.0 - lam_init)


def _softmax_update(s, v, m_ref, l_ref, acc_ref, idx):
    m_old = m_ref[idx]
    m_new = jnp.maximum(m_old, jnp.max(s, axis=-1, keepdims=True))
    a = jnp.exp(m_old - m_new)
    p = jnp.exp(s - m_new)
    l_ref[idx] = a * l_ref[idx] + jnp.sum(p, axis=-1, keepdims=True)
    acc_ref[idx] = a * acc_ref[idx] + _dot(p.astype(BF16), v)
    m_ref[idx] = m_new


KPOS_RADIX = 256
PROMPT_HEADS_PER_STEP = 4
STAT_ROWS = 8
SUM_ROWS = 16


def _attn_prompt_kernel(ti_ref, tj_ref, slopes_ref, lam_ref, q_ref, k_ref, vt_ref, g_ref, o_ref,
                        qa_ref, m_ref, acc_ref, *, tq, hd, hp, lam_init):
    hg, pair = pl.program_id(1), pl.program_id(2)
    i, j = ti_ref[pair], tj_ref[pair]
    vd = 2 * hd

    @pl.when(j == 0)
    def _():
        m_ref[...] = jnp.full_like(m_ref, NEG_INF)
        acc_ref[...] = jnp.zeros_like(acc_ref)
        lane = lax.broadcasted_iota(jnp.int32, (tq, vd), 1)
        for hh in range(hp):
            q = q_ref[:, hh * vd:(hh + 1) * vd]
            slope = slopes_ref[hg * hp + hh]
            aug = jnp.where(lane == 0, slope * KPOS_RADIX, jnp.where(lane == 1, slope, 0.0)).astype(BF16)
            qa_ref[hh, 0:tq, 0:vd] = jnp.where(lane < hd, q, jnp.zeros_like(q))
            qa_ref[hh, tq:, 0:vd] = jnp.where(lane >= hd, q, jnp.zeros_like(q))
            qa_ref[hh, 0:tq, vd:] = aug
            qa_ref[hh, tq:, vd:] = aug

    def block(diagonal):
        kpos = lax.broadcasted_iota(jnp.int32, (tq, vd), 0) + j * tq
        lane = lax.broadcasted_iota(jnp.int32, (tq, vd), 1)
        kaug = jnp.where(lane == 0, kpos // KPOS_RADIX, jnp.where(lane == 1, kpos % KPOS_RADIX, 0))
        kaug = kaug.astype(F32).astype(BF16)
        ones = jnp.ones((SUM_ROWS, tq), BF16)
        if diagonal:
            kr = lax.broadcasted_iota(jnp.int32, (tq, 2 * tq), 0)
            qc = lax.broadcasted_iota(jnp.int32, (tq, 2 * tq), 1)
            visible = jnp.where(qc >= tq, qc - tq, qc) >= kr

        def scores(hh):
            ka = jnp.concatenate([k_ref[:, hh * vd:(hh + 1) * vd], kaug], axis=1)
            return _dot_nt(ka, qa_ref[hh])

        s_next = scores(0)
        for hh in range(hp):
            s = s_next
            if hh + 1 < hp:
                s_next = scores(hh + 1)
            if diagonal:
                s = jnp.where(visible, s, NEG_INF)
            m_old = m_ref[hh]
            m_new = jnp.maximum(m_old, jnp.max(s, axis=0, keepdims=True))
            a = jnp.exp(m_old[0:1] - m_new[0:1])
            p = jnp.exp(s - m_new[0:1]).astype(BF16)
            vt1 = jnp.concatenate([vt_ref[hh * vd:(hh + 1) * vd, :], ones], axis=0)
            acc_ref[hh] = a * acc_ref[hh] + _dot(vt1, p)
            m_ref[hh] = m_new

    @pl.when(j < i)
    def _():
        block(False)

    @pl.when(j == i)
    def _():
        block(True)
        lam = _lambda_value(lam_ref, lam_init)
        for hh in range(hp):
            acc = acc_ref[hh]
            ot = acc[:vd] / acc[vd:vd + 1]
            ot = ot[:, :tq] - lam * ot[:, tq:]
            ot = ot * lax.rsqrt(jnp.mean(ot * ot, axis=0, keepdims=True) + RMS_EPS)
            o = ot.T * g_ref[:, hh * vd:(hh + 1) * vd] * (1.0 - lam_init)
            o_ref[:, hh * vd:(hh + 1) * vd] = o.astype(o_ref.dtype)


def _attn_prompt(q, k, vt, slopes, lam_qk, subln_g, B, T, H, lam_init, hp):
    W = q.shape[1]
    vd = W // H
    hd = vd // 2
    tq = _pick_tile(T, 512)
    nq = T // tq
    assert T <= KPOS_RADIX * KPOS_RADIX and H % hp == 0
    pairs = [(i, j) for i in range(nq) for j in range(i + 1)]
    ti = jnp.asarray([p[0] for p in pairs], jnp.int32)
    tj = jnp.asarray([p[1] for p in pairs], jnp.int32)
    return pl.pallas_call(
        functools.partial(_attn_prompt_kernel, tq=tq, hd=hd, hp=hp, lam_init=lam_init),
        out_shape=jax.ShapeDtypeStruct((B * T, W), BF16),
        grid_spec=pltpu.PrefetchScalarGridSpec(
            num_scalar_prefetch=2,
            grid=(B, H // hp, len(pairs)),
            in_specs=[
                pl.BlockSpec(memory_space=pltpu.SMEM),
                pl.BlockSpec(lam_qk.shape, lambda b, h, p, ti, tj: (0, 0)),
                pl.BlockSpec((tq, hp * vd), lambda b, h, p, ti, tj: (b * nq + ti[p], h)),
                pl.BlockSpec((tq, hp * vd), lambda b, h, p, ti, tj: (b * nq + tj[p], h)),
                pl.BlockSpec((hp * vd, tq), lambda b, h, p, ti, tj: (h, b * nq + tj[p])),
                pl.BlockSpec((1, hp * vd), lambda b, h, p, ti, tj: (0, h)),
            ],
            out_specs=pl.BlockSpec((tq, hp * vd), lambda b, h, p, ti, tj: (b * nq + ti[p], h)),
            scratch_shapes=[pltpu.VMEM((hp, 2 * tq, 2 * vd), BF16), pltpu.VMEM((hp, STAT_ROWS, 2 * tq), F32),
                            pltpu.VMEM((hp, vd + SUM_ROWS, 2 * tq), F32)],
        ),
        compiler_params=_cparams(("parallel", "parallel", "arbitrary")),
        name="attn_prompt",
    )(ti, tj, slopes, lam_qk, q, k, vt, subln_g)


SAMPLE_PAGES_PER_STEP = 8


def _attn_sample_kernel(pt_ref, lam_ref, qz_ref, *refs, gp, n_steps, lam_init):
    k_refs, v_refs = refs[:gp], refs[gp:2 * gp]
    kn_ref, vn_ref, bias_ref, biasn_ref, g_ref, o_ref, m_ref, l_ref, acc_ref = refs[2 * gp:]
    st = pl.program_id(1)
    rows = k_refs[0].shape[1]

    @pl.when(st == 0)
    def _():
        m_ref[...] = jnp.full_like(m_ref, NEG_INF)
        l_ref[...] = jnp.zeros_like(l_ref)
        acc_ref[...] = jnp.zeros_like(acc_ref)

    qz = qz_ref[0]
    for step in range(n_steps):
        @pl.when(st == step)
        def _():
            ss = []
            for g in range(gp):
                pg = step * gp + g
                ss.append(_dot_nt(qz, k_refs[g][0].astype(BF16)) + bias_ref[:, pg * rows:(pg + 1) * rows])
            vb = jnp.concatenate([v_refs[g][0].astype(BF16) for g in range(gp)], axis=0)
            _softmax_update(jnp.concatenate(ss, axis=1), vb, m_ref, l_ref, acc_ref, 0)

    @pl.when(st == n_steps - 1)
    def _():
        _softmax_update(_dot_nt(qz, kn_ref[0]) + biasn_ref[...], vn_ref[0], m_ref, l_ref, acc_ref, 0)
        lam = _lambda_value(lam_ref, lam_init)
        o = acc_ref[0] / l_ref[0]
        half = o.shape[0] // 2
        o = o[:half] - lam * o[half:]
        o_ref[0] = _head_norm(o, g_ref[...], lam_init).astype(o_ref.dtype)


def _attn_sample(page_table, qz, cache_k2, cache_v2, k_new, v_new, bias, bias_new, g_rows, lam_qk, lam_init):
    Bs, R, vd = qz.shape
    n_pages = page_table.shape[1]
    gp = math.gcd(n_pages, SAMPLE_PAGES_PER_STEP)
    n_steps = n_pages // gp
    rows = cache_k2.shape[1]
    rn = k_new.shape[1]
    pt = page_table.reshape(-1)

    def page_spec(g):
        return pl.BlockSpec((1, rows, vd), lambda b, s, pt_ref: (pt_ref[b * n_pages + s * gp + g], 0, 0))

    const2 = lambda b, s, pt_ref: (0, 0)
    seq3 = lambda b, s, pt_ref: (b, 0, 0)
    return pl.pallas_call(
        functools.partial(_attn_sample_kernel, gp=gp, n_steps=n_steps, lam_init=lam_init),
        out_shape=jax.ShapeDtypeStruct((Bs, R // 2, vd), BF16),
        grid_spec=pltpu.PrefetchScalarGridSpec(
            num_scalar_prefetch=1,
            grid=(Bs, n_steps),
            in_specs=[pl.BlockSpec(lam_qk.shape, const2), pl.BlockSpec((1, R, vd), seq3)]
            + [page_spec(g) for g in range(gp)] + [page_spec(g) for g in range(gp)]
            + [pl.BlockSpec((1, rn, vd), seq3), pl.BlockSpec((1, rn, vd), seq3),
               pl.BlockSpec(bias.shape, const2), pl.BlockSpec(bias_new.shape, const2),
               pl.BlockSpec(g_rows.shape, const2)],
            out_specs=pl.BlockSpec((1, R // 2, vd), seq3),
            scratch_shapes=[pltpu.VMEM((1, R, 1), F32), pltpu.VMEM((1, R, 1), F32),
                            pltpu.VMEM((1, R, vd), F32)],
        ),
        compiler_params=_cparams(("parallel", "arbitrary")),
        name="attn_sample",
    )(pt, lam_qk, qz, *([cache_k2] * gp), *([cache_v2] * gp), k_new, v_new, bias, bias_new, g_rows)


def _alibi_slopes(H):
    return 2.0 ** (-8.0 * np.arange(1, H + 1, dtype=np.float64) / H)


def _sample_bias_tables(H, Ts, page, n_pages):
    slopes = _alibi_slopes(H)
    pos0 = n_pages * page
    hq = np.tile(np.repeat(np.arange(H), Ts), 2)
    tq = np.tile(np.arange(Ts), 2 * H)
    key = np.repeat(np.arange(n_pages * page), H)
    hk = np.tile(np.arange(H), n_pages * page)
    dist = (pos0 + tq)[:, None] - key[None, :]
    bias = np.where(hq[:, None] == hk[None, :], -slopes[hq][:, None] * dist, NEG_INF)
    tn = np.repeat(np.arange(Ts), H)
    hn = np.tile(np.arange(H), Ts)
    dn = tq[:, None] - tn[None, :]
    bias_new = np.where((hq[:, None] == hn[None, :]) & (dn >= 0), -slopes[hq][:, None] * dn, NEG_INF)
    return jnp.asarray(bias, F32), jnp.asarray(bias_new, F32)


def kernel(x_prompt, x_sample, state_pool, cache_k, cache_v, page_table, p_prompt, p_sample, pool_w, pool_scale, ln1_g, ln1_b, ln2_g, ln2_b, ln_kv_g, ln_kv_b, w_kv, w_q, lam_qk, subln_g, w_o, dense_gu, dense_down, w_router, moe_gu, moe_down, ple_proj, ple_gate):
    B, T, D = x_prompt.shape
    Bs, Ts, _ = x_sample.shape
    depth = ln1_g.shape[0]
    n_a = pool_w.shape[0]
    n_pool, page, H, vd = cache_v.shape
    hd = vd // 2
    qk_width = H * 2 * hd
    n_e = w_router.shape[-1]
    n_pages = page_table.shape[1]
    pos0 = n_pages * page
    n_p, n_s = B * T, Bs * Ts
    n_all = n_p + n_s
    alpha = (2 * depth) ** 0.25
    row = lambda a: a.reshape(1, -1)

    bf = lambda w: w.astype(BF16)
    moe_gu_bf, moe_down_bf = bf(moe_gu), bf(moe_down)
    wr_pad = jnp.pad(w_router, ((0, 0), (0, 0), (0, LANES - n_e)))

    p_all = jnp.concatenate([p_prompt.reshape(depth, n_p, -1), p_sample.reshape(depth, n_s, -1)], axis=1)
    slopes_np = _alibi_slopes(H)
    assert np.all((slopes_np * KPOS_RADIX).astype(BF16).astype(np.float64) == slopes_np * KPOS_RADIX)
    assert np.all(slopes_np.astype(BF16).astype(np.float64) == slopes_np)
    slopes = jnp.asarray(slopes_np, F32)
    bias, bias_new = _sample_bias_tables(H, Ts, page, n_pages)

    x_rows = x_prompt.reshape(n_p, D)
    x_s = x_sample
    x_all = None
    pool_p, pool_s = [], []
    kv = None
    for i in range(depth):
        if i < n_a:
            pool_p.append(x_rows[:n_p].reshape(B, T, D)[:, T - POOL_BUF:])
            pool_s.append(jnp.concatenate([state_pool[i][:, Ts:], x_s], axis=1))
            pw = bf(pool_w[i])
            x1_s = _pool_sample(state_pool[i].transpose(1, 0, 2), x_s.transpose(1, 0, 2), pw,
                                row(pool_scale[i]), row(ln1_g[i]), row(ln1_b[i]), alpha, pos0)
            x1 = _pool_prompt(x_rows, B, T, x1_s.transpose(1, 0, 2).reshape(n_s, D), pw, row(pool_scale[i]),
                              row(ln1_g[i]), row(ln1_b[i]), alpha)
        else:
            jj = i - n_a
            lam_init = 0.8 - 0.6 * math.exp(-0.3 * i)
            if kv is None:
                kv = _shared_kv(x_all, n_p, row(ln_kv_g), row(ln_kv_b), bf(w_kv), bf(w_kv[:, qk_width:].T),
                                qk_width)
            k_b, v_b, v_t = kv[4:]
            q = _proj(x_all, bf(w_q[jj]), hd ** -0.5)
            o = _attn_prompt(q, k_b, v_t, slopes, lam_qk[jj], row(subln_g[jj]), B, T, H, lam_init,
                             math.gcd(H, PROMPT_HEADS_PER_STEP))
            q_s = q[n_p:].reshape(Bs, Ts, H, vd).transpose(0, 2, 1, 3).reshape(Bs, H * Ts, vd)
            lane = jnp.arange(vd)
            qz = jnp.concatenate([jnp.where(lane < hd, q_s, 0), jnp.where(lane >= hd, q_s, 0)], axis=1)
            g_rows = jnp.repeat(subln_g[jj].reshape(H, vd), Ts, axis=0)
            o_s = _attn_sample(page_table, qz, cache_k.reshape(n_pool, page * H, 2 * hd),
                               cache_v.reshape(n_pool, page * H, vd),
                               k_b[n_p:].reshape(Bs, Ts * H, 2 * hd), v_b[n_p:].reshape(Bs, Ts * H, vd),
                               bias, bias_new, g_rows, lam_qk[jj], lam_init)
            o_s = o_s.reshape(Bs, H, Ts, vd).transpose(0, 2, 1, 3).reshape(n_s, H * vd)
            x1 = _out_ln(o, o_s, bf(w_o[jj]), x_all, row(ln1_g[i]), row(ln1_b[i]), alpha)
        tail_args = (row(ln2_g[i]), row(ln2_b[i]), p_all[i], bf(ple_proj[i]), bf(ple_gate[i]), alpha)
        if i % 2 == 0:
            x_all = _ffn(x1, bf(dense_gu[i // 2]), bf(dense_down[i // 2]), *tail_args)
        else:
            x_all = _moe(x1, wr_pad[i // 2], n_e, moe_gu_bf, moe_down_bf, i // 2, *tail_args)
        x_rows = x_all
        x_s = x_all[n_p:].reshape(Bs, Ts, D)

    k_p, v_p, k_s, v_s = kv[:4]
    return (x_all[:n_p].reshape(B, T, D), x_s, jnp.stack(pool_p), jnp.stack(pool_s),
            k_p.reshape(B, T, H, 2 * hd), v_p.reshape(B, T, H, vd),
            k_s.reshape(Bs, Ts, H, 2 * hd), v_s.reshape(Bs, Ts, H, vd))
```

```python
import functools
import math

import numpy as np
import jax
import jax.numpy as jnp
from jax import lax
from jax.experimental import pallas as pl
from jax.experimental.pallas import tpu as pltpu

F32 = jnp.float32
BF16 = jnp.bfloat16

POOL_WINDOWS = (2, 4, 8, 16)
POOL_BUF = max(POOL_WINDOWS) - 1
POOL_HALO = 16
TOP_K = 2
LN_EPS = 1e-5
RMS_EPS = 1e-5
NEG_INF = -1e30
LANES = 128
SUBLANES = 8
VMEM_LIMIT = 56 * 1024 * 1024


def _cparams(sem):
    return pltpu.CompilerParams(dimension_semantics=sem, vmem_limit_bytes=VMEM_LIMIT)


def _layer_norm(x, g, b):
    mu = jnp.mean(x, axis=-1, keepdims=True)
    xc = x - mu
    var = jnp.mean(xc * xc, axis=-1, keepdims=True)
    return xc * lax.rsqrt(var + LN_EPS) * g + b


def _sigmoid(x):
    return 1.0 / (1.0 + jnp.exp(-x))


def _dot(a, b):
    return jnp.dot(a, b, preferred_element_type=F32)


def _dot_nt(a, b):
    return lax.dot_general(a, b, (((1,), (1,)), ((), ())), preferred_element_type=F32)


def _pick_tile(n, target):
    t = min(n, target)
    while n % t:
        t //= 2
    return t


def _pool_prompt_kernel(x_ref, halo_ref, tail_ref, w_ref, scale_ref, g_ref, b_ref, o_ref, ext_ref,
                        *, tm, nt, n_main, alpha):
    step = pl.program_id(0)

    @pl.when(step < n_main)
    def _():
        i = step % nt
        x = x_ref[...]
        ext_ref[0:POOL_HALO, :] = jnp.where(i > 0, halo_ref[...], 0.0)
        ext_ref[POOL_HALO:, :] = x
        gw = x.shape[-1] // len(POOL_WINDOWS)
        pos = lax.broadcasted_iota(jnp.int32, (tm, 1), 0) + i * tm
        ys = []
        for g, w in enumerate(POOL_WINDOWS):
            lo, hi = g * gw, (g + 1) * gw
            xs = x[:, lo:hi]
            s = xs
            for k in range(1, w):
                s = s + ext_ref[POOL_HALO - k:POOL_HALO - k + tm, lo:hi]
            cnt = jnp.minimum(w, pos + 1).astype(F32)
            d = s / cnt - xs
            ys.append(_dot(d.astype(BF16), w_ref[g]))
        y = jnp.concatenate(ys, axis=-1) * scale_ref[...]
        o_ref[...] = _layer_norm(alpha * x + y, g_ref[...], b_ref[...])

    @pl.when(step >= n_main)
    def _():
        o_ref[...] = tail_ref[...]


def _pool_prompt(x, B, T, tail, w_bf, scale, g, b, alpha):
    D = x.shape[1]
    n_tail = tail.shape[0]
    tm = _pick_tile(math.gcd(T, n_tail), 512)
    nt = T // tm
    n_main = B * nt
    hb = tm // POOL_HALO
    G, gw = w_bf.shape[0], w_bf.shape[1]
    main = lambda s: jnp.minimum(s, n_main - 1)
    const2 = lambda s: (0, 0)
    return pl.pallas_call(
        functools.partial(_pool_prompt_kernel, tm=tm, nt=nt, n_main=n_main, alpha=alpha),
        out_shape=jax.ShapeDtypeStruct((B * T + n_tail, D), F32),
        grid=(n_main + n_tail // tm,),
        in_specs=[
            pl.BlockSpec((tm, D), lambda s: (main(s), 0)),
            pl.BlockSpec((POOL_HALO, D), lambda s: (jnp.maximum(main(s) * hb - 1, 0), 0)),
            pl.BlockSpec((tm, D), lambda s: (jnp.maximum(s - n_main, 0), 0)),
            pl.BlockSpec((G, gw, gw), lambda s: (0, 0, 0)),
            pl.BlockSpec((1, D), const2),
            pl.BlockSpec((1, D), const2),
            pl.BlockSpec((1, D), const2),
        ],
        out_specs=pl.BlockSpec((tm, D), lambda s: (s, 0)),
        scratch_shapes=[pltpu.VMEM((tm + POOL_HALO, D), F32)],
        compiler_params=_cparams(("arbitrary",)),
        name="pool_prompt",
    )(x, x, tail, w_bf, scale, g, b)


def _pool_sample_kernel(buf_ref, x_ref, w_ref, scale_ref, g_ref, b_ref, o_ref, *, alpha, pos0):
    nb, ts = buf_ref.shape[0], x_ref.shape[0]
    rows = [buf_ref[r] for r in range(nb)] + [x_ref[t] for t in range(ts)]
    gw = rows[0].shape[-1] // len(POOL_WINDOWS)
    for t in range(ts):
        x = rows[nb + t]
        ys = []
        for g, w in enumerate(POOL_WINDOWS):
            lo, hi = g * gw, (g + 1) * gw
            xs = x[:, lo:hi]
            s = xs
            for k in range(1, w):
                s = s + rows[nb + t - k][:, lo:hi]
            d = s / float(min(w, pos0 + t + 1)) - xs
            ys.append(_dot(d.astype(BF16), w_ref[g]))
        y = jnp.concatenate(ys, axis=-1) * scale_ref[...]
        o_ref[t] = _layer_norm(alpha * x + y, g_ref[...], b_ref[...])


def _pool_sample(buf_t, x_t, w_bf, scale, g, b, alpha, pos0):
    nb, Bs, D = buf_t.shape
    ts = x_t.shape[0]
    bs = _pick_tile(Bs, 32)
    G, gw = w_bf.shape[0], w_bf.shape[1]
    return pl.pallas_call(
        functools.partial(_pool_sample_kernel, alpha=alpha, pos0=pos0),
        out_shape=jax.ShapeDtypeStruct((ts, Bs, D), F32),
        grid=(Bs // bs,),
        in_specs=[
            pl.BlockSpec((nb, bs, D), lambda i: (0, i, 0)),
            pl.BlockSpec((ts, bs, D), lambda i: (0, i, 0)),
            pl.BlockSpec((G, gw, gw), lambda i: (0, 0, 0)),
            pl.BlockSpec((1, D), lambda i: (0, 0)),
            pl.BlockSpec((1, D), lambda i: (0, 0)),
            pl.BlockSpec((1, D), lambda i: (0, 0)),
        ],
        out_specs=pl.BlockSpec((ts, bs, D), lambda i: (0, i, 0)),
        compiler_params=_cparams(("parallel",)),
        name="pool_sample",
    )(buf_t, x_t, w_bf, scale, g, b)


def _ln_ple(x, ff, g_ref, b_ref, p, proj_ref, gw_ref, alpha):
    x2 = _layer_norm(alpha * x + ff, g_ref[...], b_ref[...])
    pp = _dot(p.astype(BF16), proj_ref[...])
    gt = _dot(x2.astype(BF16), gw_ref[...])
    return x2 + pp * _sigmoid(gt)


def _swiglu_partial(xb, wg, wu, wd):
    h = _dot(xb, wg)
    u = _dot(xb, wu)
    return _dot((h * _sigmoid(h) * u).astype(BF16), wd)


def _ffn_kernel(x_ref, wg_ref, wu_ref, wd_ref, g_ref, b_ref, pp_ref, ps_ref, proj_ref, gw_ref, o_ref,
                acc_ref, xb_ref, *, n_main, alpha):
    j = pl.program_id(1)

    @pl.when(j == 0)
    def _():
        acc_ref[...] = jnp.zeros_like(acc_ref)
        xb_ref[...] = x_ref[...].astype(BF16)

    acc_ref[...] += _swiglu_partial(xb_ref[...], wg_ref[...], wu_ref[...], wd_ref[...])

    @pl.when(j == pl.num_programs(1) - 1)
    def _():
        p = jnp.where(pl.program_id(0) < n_main, pp_ref[...], ps_ref[...])
        o_ref[...] = _ln_ple(x_ref[...], acc_ref[...], g_ref, b_ref, p, proj_ref, gw_ref, alpha)


def _ffn_tile(F, target):
    best = LANES
    for t in range(LANES, min(F, target) + 1, LANES):
        if F % t == 0:
            best = t
    return best


def _ple_specs(p_p, p_s, tm):
    n_main = p_p.shape[0] // tm
    head = lambda i, *_: (jnp.minimum(i, n_main - 1), 0)
    tail = lambda i, *_: (jnp.maximum(i - n_main, 0), 0)
    return n_main, [pl.BlockSpec((tm, p_p.shape[1]), head), pl.BlockSpec((tm, p_s.shape[1]), tail)]


def _ffn(x, w_gu, w_down, g, b, p_p, p_s, proj, gate_w, alpha):
    N, D = x.shape
    F = w_down.shape[0]
    P = p_p.shape[1]
    tm = _pick_tile(math.gcd(p_p.shape[0], p_s.shape[0]), 512)
    tf = _ffn_tile(F, 1536)
    nf = F // tf
    n_main, p_specs = _ple_specs(p_p, p_s, tm)
    return pl.pallas_call(
        functools.partial(_ffn_kernel, n_main=n_main, alpha=alpha),
        out_shape=jax.ShapeDtypeStruct((N, D), F32),
        grid=(N // tm, nf),
        in_specs=[
            pl.BlockSpec((tm, D), lambda i, j: (i, 0)),
            pl.BlockSpec((D, tf), lambda i, j: (0, j)),
            pl.BlockSpec((D, tf), lambda i, j: (0, nf + j)),
            pl.BlockSpec((tf, D), lambda i, j: (j, 0)),
            pl.BlockSpec((1, D), lambda i, j: (0, 0)),
            pl.BlockSpec((1, D), lambda i, j: (0, 0)),
            *p_specs,
            pl.BlockSpec((P, D), lambda i, j: (0, 0)),
            pl.BlockSpec((D, D), lambda i, j: (0, 0)),
        ],
        out_specs=pl.BlockSpec((tm, D), lambda i, j: (i, 0)),
        scratch_shapes=[pltpu.VMEM((tm, D), F32), pltpu.VMEM((tm, D), BF16)],
        compiler_params=_cparams(("parallel", "arbitrary")),
        name="ffn_dense",
    )(x, w_gu, w_gu, w_down, g, b, p_p, p_s, proj, gate_w)


def _router_kernel(x_ref, wr_ref, o_ref, rank_ref, run_ref, *, n_e):
    @pl.when(pl.program_id(0) == 0)
    def _():
        run_ref[...] = jnp.zeros_like(run_ref)

    lg = jnp.dot(x_ref[...], wr_ref[...], preferred_element_type=F32, precision=lax.Precision.HIGHEST)
    lane = lax.broadcasted_iota(jnp.int32, lg.shape, 1)
    lg = jnp.where(lane < n_e, lg, -jnp.inf)
    v1 = jnp.max(lg, axis=-1, keepdims=True)
    i1 = jnp.min(jnp.where(lg == v1, lane, LANES), axis=-1, keepdims=True)
    lg2 = jnp.where(lane == i1, -jnp.inf, lg)
    v2 = jnp.max(lg2, axis=-1, keepdims=True)
    i2 = jnp.min(jnp.where(lg2 == v2, lane, LANES), axis=-1, keepdims=True)
    e2 = jnp.exp(v2 - v1)
    den = 1.0 + e2
    gates = jnp.where(lane == i1, 1.0 / den, 0.0) + jnp.where(lane == i2, e2 / den, 0.0)
    o_ref[...] = gates
    tm = gates.shape[0]
    tri = lax.broadcasted_iota(jnp.int32, (tm, tm), 0) >= lax.broadcasted_iota(jnp.int32, (tm, tm), 1)
    sel = jnp.where(gates > 0.0, 1.0, 0.0).astype(BF16)
    incl = run_ref[...] + _dot(jnp.where(tri, 1.0, 0.0).astype(BF16), sel)
    rank_ref[...] = incl
    run_ref[...] = incl[tm - 1:tm, :]


def _router(x, wr_pad, n_e):
    N, D = x.shape
    tm = _pick_tile(N, 512)
    return pl.pallas_call(
        functools.partial(_router_kernel, n_e=n_e),
        out_shape=(jax.ShapeDtypeStruct((N, LANES), F32), jax.ShapeDtypeStruct((N, LANES), F32)),
        grid=(N // tm,),
        in_specs=[pl.BlockSpec((tm, D), lambda i: (i, 0)),
                  pl.BlockSpec((D, LANES), lambda i: (0, 0))],
        out_specs=(pl.BlockSpec((tm, LANES), lambda i: (i, 0)), pl.BlockSpec((tm, LANES), lambda i: (i, 0))),
        scratch_shapes=[pltpu.VMEM((1, LANES), F32)],
        compiler_params=_cparams(("arbitrary",)),
        name="router",
    )(x, wr_pad)


MOE_ROW_TILE = 1024
MOE_SUB_TILE = 512


def _route_plan(gates, incl, n_e, tm):
    N = gates.shape[0]
    n_tiles = -(-TOP_K * N // tm) + n_e
    sel = gates[:, :n_e] > 0.0
    incl = incl[:, :n_e].astype(jnp.int32)
    cnt = incl[-1]
    padded = (cnt + tm - 1) // tm * tm
    ends = jnp.cumsum(padded)
    pos = (ends - padded)[None, :] + incl - 1
    rows = n_tiles * tm
    pos_lo = jnp.minimum(jnp.min(jnp.where(sel, pos, rows), axis=1), rows - 1)
    pos_hi = jnp.max(jnp.where(sel, pos, -1), axis=1)
    pos_hi = jnp.where(pos_hi < 0, pos_lo, pos_hi)
    tile_start = jnp.arange(n_tiles, dtype=jnp.int32) * tm
    tile_e = jnp.minimum(jnp.sum(tile_start[:, None] >= ends[None, :], axis=1), n_e - 1).astype(jnp.int32)
    used = jnp.clip(jnp.take(ends - padded + cnt, tile_e) - tile_start, 0, tm)
    used = jnp.where(tile_start < ends[-1], used, 0).astype(jnp.int32)
    last_e = jnp.max(jnp.where(used > 0, tile_e, 0))
    tile_e = jnp.where(tile_start < ends[-1], tile_e, last_e)
    tails = jnp.concatenate([ends - padded + cnt, ends[-1:] // tm]).astype(jnp.int32)
    return pos_lo.astype(jnp.int32), pos_hi.astype(jnp.int32), tile_e, used, tails, rows


def _row_copy(src_ref, src_row, dst_ref, dst_row, sem):
    return pltpu.make_async_copy(src_ref.at[pl.ds(src_row, 1)], dst_ref.at[pl.ds(dst_row, 1)], sem)


def _dispatch_kernel(lo_ref, hi_ref, tails_ref, x_ref, xs_ref, zero_ref, sem, *, tm, n_e):
    base = pl.program_id(0) * tm

    @pl.when(pl.program_id(0) == 0)
    def _():
        zero_ref[...] = jnp.zeros_like(zero_ref)
        fills = []
        for e in range(n_e):
            first = pl.multiple_of(tails_ref[e] // SUBLANES * SUBLANES, SUBLANES)
            fills.append(pltpu.make_async_copy(zero_ref, xs_ref.at[pl.ds(first, zero_ref.shape[0])], sem))
        for f in fills:
            f.start()
        for f in fills:
            f.wait()

        n_fill = zero_ref.shape[0]
        row_tile = n_fill - SUBLANES
        n_tiles = (xs_ref.shape[0] - n_fill) // row_tile

        def zero_tile(t, c):
            start = pl.multiple_of(t * row_tile, row_tile)
            blank = pltpu.make_async_copy(zero_ref.at[pl.ds(0, row_tile)], xs_ref.at[pl.ds(start, row_tile)], sem)
            blank.start()
            blank.wait()
            return c

        lax.fori_loop(tails_ref[n_e], n_tiles, zero_tile, 0)
        spare = pltpu.make_async_copy(zero_ref, xs_ref.at[pl.ds(n_tiles * row_tile, n_fill)], sem)
        spare.start()
        spare.wait()

    def start(r, c):
        _row_copy(x_ref, r, xs_ref, lo_ref[base + r], sem).start()
        _row_copy(x_ref, r, xs_ref, hi_ref[base + r], sem).start()
        return c

    lax.fori_loop(0, tm, start, 0)
    for _ in range(TOP_K):
        pltpu.make_async_copy(x_ref, xs_ref.at[pl.ds(0, tm)], sem).wait()


def _dispatch(pos_lo, pos_hi, tails, x, rows, row_tile):
    N, D = x.shape
    tm = _pick_tile(N, 256)
    fill = row_tile + SUBLANES
    return pl.pallas_call(
        functools.partial(_dispatch_kernel, tm=tm, n_e=tails.shape[0] - 1),
        out_shape=jax.ShapeDtypeStruct((rows + fill, D), F32),
        grid_spec=pltpu.PrefetchScalarGridSpec(
            num_scalar_prefetch=3,
            grid=(N // tm,),
            in_specs=[pl.BlockSpec((tm, D), lambda i, lo, hi, tl: (i, 0))],
            out_specs=pl.BlockSpec(memory_space=pl.ANY),
            scratch_shapes=[pltpu.VMEM((fill, D), F32), pltpu.SemaphoreType.DMA],
        ),
        compiler_params=_cparams(("arbitrary",)),
        name="moe_dispatch",
    )(pos_lo, pos_hi, tails, x)


def _experts_kernel(te_ref, used_ref, x_ref, wg_ref, wu_ref, wd_ref, o_ref, acc_ref, xb_ref, *, sub):
    r, j = pl.program_id(0), pl.program_id(1)
    used = used_ref[r]
    tm = x_ref.shape[0]
    n_sub = tm // sub

    @pl.when(j == 0)
    def _():
        acc_ref[...] = jnp.zeros_like(acc_ref)

    @pl.when((j == 0) & (used > 0))
    def _():
        xb_ref[...] = x_ref[...].astype(BF16)

    @pl.when(used > tm - sub)
    def _():
        acc_ref[...] += _swiglu_partial(xb_ref[...], wg_ref[...], wu_ref[...], wd_ref[...])

    for s in range(n_sub - 1):
        @pl.when((used > s * sub) & (used <= tm - sub))
        def _():
            rs = pl.ds(s * sub, sub)
            acc_ref[rs, :] += _swiglu_partial(xb_ref[rs, :], wg_ref[...], wu_ref[...], wd_ref[...])

    @pl.when(j == pl.num_programs(1) - 1)
    def _():
        o_ref[...] = acc_ref[...]


def _experts(tile_e, used, xs, w_gu, w_down, layer, tm):
    D = xs.shape[1]
    rows = tile_e.shape[0] * tm
    F = w_down.shape[2]
    tf = _ffn_tile(F, 512)
    nf = F // tf

    def jmap(r, j, used_ref):
        return jnp.where(used_ref[r] > 0, j, nf - 1)

    return pl.pallas_call(
        functools.partial(_experts_kernel, sub=min(MOE_SUB_TILE, tm)),
        out_shape=jax.ShapeDtypeStruct((rows, D), F32),
        grid_spec=pltpu.PrefetchScalarGridSpec(
            num_scalar_prefetch=2,
            grid=(rows // tm, nf),
            in_specs=[
                pl.BlockSpec((tm, D), lambda r, j, te, us: (r, 0)),
                pl.BlockSpec((None, None, D, tf), lambda r, j, te, us: (layer, te[r], 0, jmap(r, j, us))),
                pl.BlockSpec((None, None, D, tf), lambda r, j, te, us: (layer, te[r], 0, nf + jmap(r, j, us))),
                pl.BlockSpec((None, None, tf, D), lambda r, j, te, us: (layer, te[r], jmap(r, j, us), 0)),
            ],
            out_specs=pl.BlockSpec((tm, D), lambda r, j, te, us: (r, 0)),
            scratch_shapes=[pltpu.VMEM((tm, D), F32), pltpu.VMEM((tm, D), BF16)],
        ),
        compiler_params=_cparams(("parallel", "arbitrary")),
        name="moe_experts",
    )(tile_e, used, xs, w_gu, w_gu, w_down)


def _combine_kernel(lo_ref, hi_ref, x_ref, gates_ref, ys_ref, g_ref, b_ref, pp_ref, ps_ref, proj_ref, gw_ref,
                    o_ref, ya_ref, yb_ref, sems, *, tm, n_main, alpha):
    step = pl.program_id(0)
    slot = step % 2

    last = pl.num_programs(0) - 1

    def start(tile, dst_slot, r):
        _row_copy(ys_ref, lo_ref[tile * tm + r], ya_ref.at[dst_slot], r, sems.at[dst_slot]).start()
        _row_copy(ys_ref, hi_ref[tile * tm + r], yb_ref.at[dst_slot], r, sems.at[dst_slot]).start()

    def wait(w_slot):
        pltpu.make_async_copy(ys_ref.at[pl.ds(0, tm)], ya_ref.at[w_slot], sems.at[w_slot]).wait()
        pltpu.make_async_copy(ys_ref.at[pl.ds(0, tm)], yb_ref.at[w_slot], sems.at[w_slot]).wait()

    @pl.when(step == 0)
    def _():
        lax.fori_loop(0, tm, lambda r, c: (start(step, slot, r), c)[1], 0)

    nxt = jnp.minimum(step + 1, last)
    for r in range(tm):
        start(nxt, 1 - slot, r)

    gates = gates_ref[...]
    lane = lax.broadcasted_iota(jnp.int32, gates.shape, 1)
    sel = gates > 0.0
    e_lo = jnp.min(jnp.where(sel, lane, LANES), axis=-1, keepdims=True)
    e_hi = jnp.max(jnp.where(sel, lane, -1), axis=-1, keepdims=True)
    g_lo = jnp.sum(jnp.where(lane == e_lo, gates, 0.0), axis=-1, keepdims=True)
    g_hi = jnp.where(e_hi != e_lo, jnp.sum(jnp.where(lane == e_hi, gates, 0.0), axis=-1, keepdims=True), 0.0)
    wait(slot)
    ff = g_lo * ya_ref[slot] + g_hi * yb_ref[slot]
    p = jnp.where(step < n_main, pp_ref[...], ps_ref[...])
    o_ref[...] = _ln_ple(x_ref[...], ff, g_ref, b_ref, p, proj_ref, gw_ref, alpha)

    @pl.when(step == last)
    def _():
        wait(1 - slot)


def _combine(pos_lo, pos_hi, x, gates, ys, g, b, p_p, p_s, proj, gate_w, alpha):
    N, D = x.shape
    P = p_p.shape[1]
    tm = _pick_tile(math.gcd(p_p.shape[0], p_s.shape[0]), 256)
    row = lambda i, lo, hi: (i, 0)
    const = lambda i, lo, hi: (0, 0)
    n_main, p_specs = _ple_specs(p_p, p_s, tm)
    return pl.pallas_call(
        functools.partial(_combine_kernel, tm=tm, n_main=n_main, alpha=alpha),
        out_shape=jax.ShapeDtypeStruct((N, D), F32),
        grid_spec=pltpu.PrefetchScalarGridSpec(
            num_scalar_prefetch=2,
            grid=(N // tm,),
            in_specs=[pl.BlockSpec((tm, D), row),
                      pl.BlockSpec((tm, LANES), row),
                      pl.BlockSpec(memory_space=pl.ANY),
                      pl.BlockSpec((1, D), const),
                      pl.BlockSpec((1, D), const),
                      *p_specs,
                      pl.BlockSpec((P, D), const),
                      pl.BlockSpec((D, D), const)],
            out_specs=pl.BlockSpec((tm, D), row),
            scratch_shapes=[pltpu.VMEM((2, tm, D), F32), pltpu.VMEM((2, tm, D), F32),
                            pltpu.SemaphoreType.DMA((2,))],
        ),
        compiler_params=_cparams(("arbitrary",)),
        name="moe_combine",
    )(pos_lo, pos_hi, x, gates, ys, g, b, p_p, p_s, proj, gate_w)


def _moe(x, wr_pad, n_e, w_gu, w_down, layer, g, b, p_p, p_s, proj, gate_w, alpha):
    gates, incl = _router(x, wr_pad, n_e)
    tm = MOE_ROW_TILE
    pos_lo, pos_hi, tile_e, used, tails, rows = _route_plan(gates, incl, n_e, tm)
    xs = _dispatch(pos_lo, pos_hi, tails, x, rows, tm)
    ys = _experts(tile_e, used, xs, w_gu, w_down, layer, tm)
    return _combine(pos_lo, pos_hi, x, gates, ys, g, b, p_p, p_s, proj, gate_w, alpha)


def _kv_kernel(x_ref, g_ref, b_ref, w_ref, wvt_ref, kp_ref, vp_ref, ks_ref, vs_ref, kb_ref, vb_ref, vt_ref,
               *, n_main):
    xn = _layer_norm(x_ref[...], g_ref[...], b_ref[...]).astype(BF16)
    qk = kp_ref.shape[-1]
    k = _dot(xn, w_ref[:, :qk])
    v = _dot(xn, w_ref[:, qk:])

    @pl.when(pl.program_id(0) < n_main)
    def _():
        kp_ref[...] = k
        vp_ref[...] = v

    @pl.when(pl.program_id(0) >= n_main)
    def _():
        ks_ref[...] = k
        vs_ref[...] = v

    kb_ref[...] = k.astype(BF16)
    vb_ref[...] = v.astype(BF16)
    vt_ref[...] = _dot_nt(wvt_ref[...], xn).astype(BF16)


def _shared_kv(x, n_p, g, b, w_kv_bf, w_vt_bf, qk_width):
    N, D = x.shape
    n_s = N - n_p
    W = w_kv_bf.shape[1]
    vw = W - qk_width
    tm = _pick_tile(math.gcd(n_p, n_s), 512)
    n_main = n_p // tm
    row = lambda i: (i, 0)
    const = lambda i: (0, 0)
    head = lambda i: (jnp.minimum(i, n_main - 1), 0)
    tail = lambda i: (jnp.maximum(i - n_main, 0), 0)
    return pl.pallas_call(
        functools.partial(_kv_kernel, n_main=n_main),
        out_shape=(jax.ShapeDtypeStruct((n_p, qk_width), F32), jax.ShapeDtypeStruct((n_p, vw), F32),
                   jax.ShapeDtypeStruct((n_s, qk_width), F32), jax.ShapeDtypeStruct((n_s, vw), F32),
                   jax.ShapeDtypeStruct((N, qk_width), BF16), jax.ShapeDtypeStruct((N, vw), BF16),
                   jax.ShapeDtypeStruct((vw, N), BF16)),
        grid=(N // tm,),
        in_specs=[pl.BlockSpec((tm, D), row), pl.BlockSpec((1, D), const), pl.BlockSpec((1, D), const),
                  pl.BlockSpec((D, W), const), pl.BlockSpec((vw, D), const)],
        out_specs=(pl.BlockSpec((tm, qk_width), head), pl.BlockSpec((tm, vw), head),
                   pl.BlockSpec((tm, qk_width), tail), pl.BlockSpec((tm, vw), tail),
                   pl.BlockSpec((tm, qk_width), row), pl.BlockSpec((tm, vw), row),
                   pl.BlockSpec((vw, tm), lambda i: (0, i))),
        compiler_params=_cparams(("arbitrary",)),
        name="shared_kv",
    )(x, g, b, w_kv_bf, w_vt_bf)


def _proj_kernel(x_ref, w_ref, o_ref, *, scale):
    o_ref[...] = (_dot(x_ref[...].astype(BF16), w_ref[...]) * scale).astype(o_ref.dtype)


def _proj(x, w_bf, scale):
    N, D = x.shape
    W = w_bf.shape[1]
    tm = _pick_tile(N, 512)
    return pl.pallas_call(
        functools.partial(_proj_kernel, scale=scale),
        out_shape=jax.ShapeDtypeStruct((N, W), BF16),
        grid=(N // tm,),
        in_specs=[pl.BlockSpec((tm, D), lambda i: (i, 0)),
                  pl.BlockSpec((D, W), lambda i: (0, 0))],
        out_specs=pl.BlockSpec((tm, W), lambda i: (i, 0)),
        compiler_params=_cparams(("parallel",)),
        name="q_proj",
    )(x, w_bf)


def _out_ln_kernel(op_ref, os_ref, w_ref, x_ref, g_ref, b_ref, y_ref, *, n_main, alpha):
    o = jnp.where(pl.program_id(0) < n_main, op_ref[...], os_ref[...])
    y_ref[...] = _layer_norm(alpha * x_ref[...] + _dot(o, w_ref[...]), g_ref[...], b_ref[...])


def _out_ln(o_p, o_s, w_bf, x, g, b, alpha):
    N, D = x.shape
    W = o_p.shape[1]
    tm = _pick_tile(math.gcd(o_p.shape[0], o_s.shape[0]), 512)
    n_main = o_p.shape[0] // tm
    return pl.pallas_call(
        functools.partial(_out_ln_kernel, n_main=n_main, alpha=alpha),
        out_shape=jax.ShapeDtypeStruct((N, D), F32),
        grid=(N // tm,),
        in_specs=[pl.BlockSpec((tm, W), lambda i: (jnp.minimum(i, n_main - 1), 0)),
                  pl.BlockSpec((tm, W), lambda i: (jnp.maximum(i - n_main, 0), 0)),
                  pl.BlockSpec((W, D), lambda i: (0, 0)),
                  pl.BlockSpec((tm, D), lambda i: (i, 0)),
                  pl.BlockSpec((1, D), lambda i: (0, 0)),
                  pl.BlockSpec((1, D), lambda i: (0, 0))],
        out_specs=pl.BlockSpec((tm, D), lambda i: (i, 0)),
        compiler_params=_cparams(("parallel",)),
        name="attn_out_ln",
    )(o_p, o_s, w_bf, x, g, b)


def _lambda_value(lam_ref, lam_init):
    lq = lam_ref[...]
    s1 = jnp.sum(lq[0:1] * lq[1:2], axis=-1, keepdims=True)
    s2 = jnp.sum(lq[2:3] * lq[3:4], axis=-1, keepdims=True)
    return jnp.exp(s1) - jnp.exp(s2) + lam_init


def _head_norm(o, g, lam_init):
    return o * lax.rsqrt(jnp.mean(o * o, axis=-1, keepdims=True) + RMS_EPS) * g * (1.0 - lam_init)


def _softmax_update(s, v, m_ref, l_ref, acc_ref, idx):
    m_old = m_ref[idx]
    m_new = jnp.maximum(m_old, jnp.max(s, axis=-1, keepdims=True))
    a = jnp.exp(m_old - m_new)
    p = jnp.exp(s - m_new)
    l_ref[idx] = a * l_ref[idx] + jnp.sum(p, axis=-1, keepdims=True)
    acc_ref[idx] = a * acc_ref[idx] + _dot(p.astype(BF16), v)
    m_ref[idx] = m_new


KPOS_RADIX = 256
PROMPT_HEADS_PER_STEP = 4
STAT_ROWS = 8
SUM_ROWS = 16


def _attn_prompt_kernel(ti_ref, tj_ref, slopes_ref, lam_ref, q_ref, k_ref, vt_ref, g_ref, o_ref,
                        qa_ref, m_ref, acc_ref, *, tq, hd, hp, lam_init):
    hg, pair = pl.program_id(1), pl.program_id(2)
    i, j = ti_ref[pair], tj_ref[pair]
    vd = 2 * hd

    @pl.when(j == 0)
    def _():
        m_ref[...] = jnp.full_like(m_ref, NEG_INF)
        acc_ref[...] = jnp.zeros_like(acc_ref)
        lane = lax.broadcasted_iota(jnp.int32, (tq, vd), 1)
        for hh in range(hp):
            q = q_ref[:, hh * vd:(hh + 1) * vd]
            slope = slopes_ref[hg * hp + hh]
            aug = jnp.where(lane == 0, slope * KPOS_RADIX, jnp.where(lane == 1, slope, 0.0)).astype(BF16)
            qa_ref[hh, 0:tq, 0:vd] = jnp.where(lane < hd, q, jnp.zeros_like(q))
            qa_ref[hh, tq:, 0:vd] = jnp.where(lane >= hd, q, jnp.zeros_like(q))
            qa_ref[hh, 0:tq, vd:] = aug
            qa_ref[hh, tq:, vd:] = aug

    def block(diagonal):
        kpos = lax.broadcasted_iota(jnp.int32, (tq, vd), 0) + j * tq
        lane = lax.broadcasted_iota(jnp.int32, (tq, vd), 1)
        kaug = jnp.where(lane == 0, kpos // KPOS_RADIX, jnp.where(lane == 1, kpos % KPOS_RADIX, 0))
        kaug = kaug.astype(F32).astype(BF16)
        ones = jnp.ones((SUM_ROWS, tq), BF16)
        if diagonal:
            kr = lax.broadcasted_iota(jnp.int32, (tq, 2 * tq), 0)
            qc = lax.broadcasted_iota(jnp.int32, (tq, 2 * tq), 1)
            visible = jnp.where(qc >= tq, qc - tq, qc) >= kr

        def scores(hh):
            ka = jnp.concatenate([k_ref[:, hh * vd:(hh + 1) * vd], kaug], axis=1)
            return _dot_nt(ka, qa_ref[hh])

        s_next = scores(0)
        for hh in range(hp):
            s = s_next
            if hh + 1 < hp:
                s_next = scores(hh + 1)
            if diagonal:
                s = jnp.where(visible, s, NEG_INF)
            m_old = m_ref[hh]
            m_new = jnp.maximum(m_old, jnp.max(s, axis=0, keepdims=True))
            a = jnp.exp(m_old[0:1] - m_new[0:1])
            p = jnp.exp(s - m_new[0:1]).astype(BF16)
            vt1 = jnp.concatenate([vt_ref[hh * vd:(hh + 1) * vd, :], ones], axis=0)
            acc_ref[hh] = a * acc_ref[hh] + _dot(vt1, p)
            m_ref[hh] = m_new

    @pl.when(j < i)
    def _():
        block(False)

    @pl.when(j == i)
    def _():
        block(True)
        lam = _lambda_value(lam_ref, lam_init)
        for hh in range(hp):
            acc = acc_ref[hh]
            ot = acc[:vd] / acc[vd:vd + 1]
            ot = ot[:, :tq] - lam * ot[:, tq:]
            ot = ot * lax.rsqrt(jnp.mean(ot * ot, axis=0, keepdims=True) + RMS_EPS)
            o = ot.T * g_ref[:, hh * vd:(hh + 1) * vd] * (1.0 - lam_init)
            o_ref[:, hh * vd:(hh + 1) * vd] = o.astype(o_ref.dtype)


def _attn_prompt(q, k, vt, slopes, lam_qk, subln_g, B, T, H, lam_init, hp):
    W = q.shape[1]
    vd = W // H
    hd = vd // 2
    tq = _pick_tile(T, 512)
    nq = T // tq
    assert T <= KPOS_RADIX * KPOS_RADIX and H % hp == 0
    pairs = [(i, j) for i in range(nq) for j in range(i + 1)]
    ti = jnp.asarray([p[0] for p in pairs], jnp.int32)
    tj = jnp.asarray([p[1] for p in pairs], jnp.int32)
    return pl.pallas_call(
        functools.partial(_attn_prompt_kernel, tq=tq, hd=hd, hp=hp, lam_init=lam_init),
        out_shape=jax.ShapeDtypeStruct((B * T, W), BF16),
        grid_spec=pltpu.PrefetchScalarGridSpec(
            num_scalar_prefetch=2,
            grid=(B, H // hp, len(pairs)),
            in_specs=[
                pl.BlockSpec(memory_space=pltpu.SMEM),
                pl.BlockSpec(lam_qk.shape, lambda b, h, p, ti, tj: (0, 0)),
                pl.BlockSpec((tq, hp * vd), lambda b, h, p, ti, tj: (b * nq + ti[p], h)),
                pl.BlockSpec((tq, hp * vd), lambda b, h, p, ti, tj: (b * nq + tj[p], h)),
                pl.BlockSpec((hp * vd, tq), lambda b, h, p, ti, tj: (h, b * nq + tj[p])),
                pl.BlockSpec((1, hp * vd), lambda b, h, p, ti, tj: (0, h)),
            ],
            out_specs=pl.BlockSpec((tq, hp * vd), lambda b, h, p, ti, tj: (b * nq + ti[p], h)),
            scratch_shapes=[pltpu.VMEM((hp, 2 * tq, 2 * vd), BF16), pltpu.VMEM((hp, STAT_ROWS, 2 * tq), F32),
                            pltpu.VMEM((hp, vd + SUM_ROWS, 2 * tq), F32)],
        ),
        compiler_params=_cparams(("parallel", "parallel", "arbitrary")),
        name="attn_prompt",
    )(ti, tj, slopes, lam_qk, q, k, vt, subln_g)


SAMPLE_PAGES_PER_STEP = 8


def _attn_sample_kernel(pt_ref, lam_ref, qz_ref, *refs, gp, n_steps, lam_init):
    k_refs, v_refs = refs[:gp], refs[gp:2 * gp]
    kn_ref, vn_ref, bias_ref, biasn_ref, g_ref, o_ref, m_ref, l_ref, acc_ref = refs[2 * gp:]
    st = pl.program_id(1)
    rows = k_refs[0].shape[1]

    @pl.when(st == 0)
    def _():
        m_ref[...] = jnp.full_like(m_ref, NEG_INF)
        l_ref[...] = jnp.zeros_like(l_ref)
        acc_ref[...] = jnp.zeros_like(acc_ref)

    qz = qz_ref[0]
    for step in range(n_steps):
        @pl.when(st == step)
        def _():
            ss = []
            for g in range(gp):
                pg = step * gp + g
                ss.append(_dot_nt(qz, k_refs[g][0].astype(BF16)) + bias_ref[:, pg * rows:(pg + 1) * rows])
            vb = jnp.concatenate([v_refs[g][0].astype(BF16) for g in range(gp)], axis=0)
            _softmax_update(jnp.concatenate(ss, axis=1), vb, m_ref, l_ref, acc_ref, 0)

    @pl.when(st == n_steps - 1)
    def _():
        _softmax_update(_dot_nt(qz, kn_ref[0]) + biasn_ref[...], vn_ref[0], m_ref, l_ref, acc_ref, 0)
        lam = _lambda_value(lam_ref, lam_init)
        o = acc_ref[0] / l_ref[0]
        half = o.shape[0] // 2
        o = o[:half] - lam * o[half:]
        o_ref[0] = _head_norm(o, g_ref[...], lam_init).astype(o_ref.dtype)


def _attn_sample(page_table, qz, cache_k2, cache_v2, k_new, v_new, bias, bias_new, g_rows, lam_qk, lam_init):
    Bs, R, vd = qz.shape
    n_pages = page_table.shape[1]
    gp = math.gcd(n_pages, SAMPLE_PAGES_PER_STEP)
    n_steps = n_pages // gp
    rows = cache_k2.shape[1]
    rn = k_new.shape[1]
    pt = page_table.reshape(-1)

    def page_spec(g):
        return pl.BlockSpec((1, rows, vd), lambda b, s, pt_ref: (pt_ref[b * n_pages + s * gp + g], 0, 0))

    const2 = lambda b, s, pt_ref: (0, 0)
    seq3 = lambda b, s, pt_ref: (b, 0, 0)
    return pl.pallas_call(
        functools.partial(_attn_sample_kernel, gp=gp, n_steps=n_steps, lam_init=lam_init),
        out_shape=jax.ShapeDtypeStruct((Bs, R // 2, vd), BF16),
        grid_spec=pltpu.PrefetchScalarGridSpec(
            num_scalar_prefetch=1,
            grid=(Bs, n_steps),
            in_specs=[pl.BlockSpec(lam_qk.shape, const2), pl.BlockSpec((1, R, vd), seq3)]
            + [page_spec(g) for g in range(gp)] + [page_spec(g) for g in range(gp)]
            + [pl.BlockSpec((1, rn, vd), seq3), pl.BlockSpec((1, rn, vd), seq3),
               pl.BlockSpec(bias.shape, const2), pl.BlockSpec(bias_new.shape, const2),
               pl.BlockSpec(g_rows.shape, const2)],
            out_specs=pl.BlockSpec((1, R // 2, vd), seq3),
            scratch_shapes=[pltpu.VMEM((1, R, 1), F32), pltpu.VMEM((1, R, 1), F32),
                            pltpu.VMEM((1, R, vd), F32)],
        ),
        compiler_params=_cparams(("parallel", "arbitrary")),
        name="attn_sample",
    )(pt, lam_qk, qz, *([cache_k2] * gp), *([cache_v2] * gp), k_new, v_new, bias, bias_new, g_rows)


def _alibi_slopes(H):
    return 2.0 ** (-8.0 * np.arange(1, H + 1, dtype=np.float64) / H)


def _sample_bias_tables(H, Ts, page, n_pages):
    slopes = _alibi_slopes(H)
    pos0 = n_pages * page
    hq = np.tile(np.repeat(np.arange(H), Ts), 2)
    tq = np.tile(np.arange(Ts), 2 * H)
    key = np.repeat(np.arange(n_pages * page), H)
    hk = np.tile(np.arange(H), n_pages * page)
    dist = (pos0 + tq)[:, None] - key[None, :]
    bias = np.where(hq[:, None] == hk[None, :], -slopes[hq][:, None] * dist, NEG_INF)
    tn = np.repeat(np.arange(Ts), H)
    hn = np.tile(np.arange(H), Ts)
    dn = tq[:, None] - tn[None, :]
    bias_new = np.where((hq[:, None] == hn[None, :]) & (dn >= 0), -slopes[hq][:, None] * dn, NEG_INF)
    return jnp.asarray(bias, F32), jnp.asarray(bias_new, F32)


def kernel(x_prompt, x_sample, state_pool, cache_k, cache_v, page_table, p_prompt, p_sample, pool_w, pool_scale, ln1_g, ln1_b, ln2_g, ln2_b, ln_kv_g, ln_kv_b, w_kv, w_q, lam_qk, subln_g, w_o, dense_gu, dense_down, w_router, moe_gu, moe_down, ple_proj, ple_gate):
    B, T, D = x_prompt.shape
    Bs, Ts, _ = x_sample.shape
    depth = ln1_g.shape[0]
    n_a = pool_w.shape[0]
    n_pool, page, H, vd = cache_v.shape
    hd = vd // 2
    qk_width = H * 2 * hd
    n_e = w_router.shape[-1]
    n_pages = page_table.shape[1]
    pos0 = n_pages * page
    n_p, n_s = B * T, Bs * Ts
    n_all = n_p + n_s
    alpha = (2 * depth) ** 0.25
    row = lambda a: a.reshape(1, -1)

    bf = lambda w: w.astype(BF16)
    moe_gu_bf, moe_down_bf = bf(moe_gu), bf(moe_down)
    wr_pad = jnp.pad(w_router, ((0, 0), (0, 0), (0, LANES - n_e)))

    p_p, p_s = p_prompt.reshape(depth, n_p, -1), p_sample.reshape(depth, n_s, -1)
    slopes_np = _alibi_slopes(H)
    assert np.all((slopes_np * KPOS_RADIX).astype(BF16).astype(np.float64) == slopes_np * KPOS_RADIX)
    assert np.all(slopes_np.astype(BF16).astype(np.float64) == slopes_np)
    slopes = jnp.asarray(slopes_np, F32)
    bias, bias_new = _sample_bias_tables(H, Ts, page, n_pages)

    x_rows = x_prompt.reshape(n_p, D)
    x_s = x_sample
    x_all = None
    pool_p, pool_s = [], []
    kv = None
    for i in range(depth):
        if i < n_a:
            pool_p.append(jnp.stack([x_rows[(bb + 1) * T - POOL_BUF:(bb + 1) * T] for bb in range(B)]))
            pool_s.append(jnp.concatenate([state_pool[i][:, Ts:], x_s], axis=1))
            pw = bf(pool_w[i])
            x1_s = _pool_sample(state_pool[i].transpose(1, 0, 2), x_s.transpose(1, 0, 2), pw,
                                row(pool_scale[i]), row(ln1_g[i]), row(ln1_b[i]), alpha, pos0)
            x1 = _pool_prompt(x_rows, B, T, x1_s.transpose(1, 0, 2).reshape(n_s, D), pw, row(pool_scale[i]),
                              row(ln1_g[i]), row(ln1_b[i]), alpha)
        else:
            jj = i - n_a
            lam_init = 0.8 - 0.6 * math.exp(-0.3 * i)
            if kv is None:
                kv = _shared_kv(x_all, n_p, row(ln_kv_g), row(ln_kv_b), bf(w_kv), bf(w_kv[:, qk_width:].T),
                                qk_width)
            k_b, v_b, v_t = kv[4:]
            q = _proj(x_all, bf(w_q[jj]), hd ** -0.5)
            o = _attn_prompt(q, k_b, v_t, slopes, lam_qk[jj], row(subln_g[jj]), B, T, H, lam_init,
                             math.gcd(H, PROMPT_HEADS_PER_STEP))
            q_s = q[n_p:].reshape(Bs, Ts, H, vd).transpose(0, 2, 1, 3).reshape(Bs, H * Ts, vd)
            lane = jnp.arange(vd)
            qz = jnp.concatenate([jnp.where(lane < hd, q_s, 0), jnp.where(lane >= hd, q_s, 0)], axis=1)
            g_rows = jnp.repeat(subln_g[jj].reshape(H, vd), Ts, axis=0)
            o_s = _attn_sample(page_table, qz, cache_k.reshape(n_pool, page * H, 2 * hd),
                               cache_v.reshape(n_pool, page * H, vd),
                               k_b[n_p:].reshape(Bs, Ts * H, 2 * hd), v_b[n_p:].reshape(Bs, Ts * H, vd),
                               bias, bias_new, g_rows, lam_qk[jj], lam_init)
            o_s = o_s.reshape(Bs, H, Ts, vd).transpose(0, 2, 1, 3).reshape(n_s, H * vd)
            x1 = _out_ln(o, o_s, bf(w_o[jj]), x_all, row(ln1_g[i]), row(ln1_b[i]), alpha)
        tail_args = (row(ln2_g[i]), row(ln2_b[i]), p_p[i], p_s[i], bf(ple_proj[i]), bf(ple_gate[i]), alpha)
        if i % 2 == 0:
            x_all = _ffn(x1, bf(dense_gu[i // 2]), bf(dense_down[i // 2]), *tail_args)
        else:
            x_all = _moe(x1, wr_pad[i // 2], n_e, moe_gu_bf, moe_down_bf, i // 2, *tail_args)
        x_rows = x_all
        x_s = x_all[n_p:].reshape(Bs, Ts, D)

    k_p, v_p, k_s, v_s = kv[:4]
    return (x_all[:n_p].reshape(B, T, D), x_s, jnp.stack(pool_p), jnp.stack(pool_s),
            k_p.reshape(B, T, H, 2 * hd), v_p.reshape(B, T, H, vd),
            k_s.reshape(Bs, Ts, H, 2 * hd), v_s.reshape(Bs, Ts, H, vd))
```

```python
import functools
import math

import numpy as np
import jax
import jax.numpy as jnp
from jax import lax
from jax.experimental import pallas as pl
from jax.experimental.pallas import tpu as pltpu

F32 = jnp.float32
BF16 = jnp.bfloat16

POOL_WINDOWS = (2, 4, 8, 16)
POOL_BUF = max(POOL_WINDOWS) - 1
POOL_HALO = 16
TOP_K = 2
LN_EPS = 1e-5
RMS_EPS = 1e-5
NEG_INF = -1e30
LANES = 128
SUBLANES = 8
VMEM_LIMIT = 56 * 1024 * 1024


def _cparams(sem):
    return pltpu.CompilerParams(dimension_semantics=sem, vmem_limit_bytes=VMEM_LIMIT)


def _layer_norm(x, g, b):
    mu = jnp.mean(x, axis=-1, keepdims=True)
    xc = x - mu
    var = jnp.mean(xc * xc, axis=-1, keepdims=True)
    return xc * lax.rsqrt(var + LN_EPS) * g + b


def _sigmoid(x):
    return 1.0 / (1.0 + jnp.exp(-x))


def _dot(a, b):
    return jnp.dot(a, b, preferred_element_type=F32)


def _dot_nt(a, b):
    return lax.dot_general(a, b, (((1,), (1,)), ((), ())), preferred_element_type=F32)


def _pick_tile(n, target):
    t = min(n, target)
    while n % t:
        t //= 2
    return t


def _pool_prompt_kernel(x_ref, halo_ref, tail_ref, w_ref, scale_ref, g_ref, b_ref, o_ref, ext_ref,
                        *, tm, nt, n_main, alpha):
    step = pl.program_id(0)

    @pl.when(step < n_main)
    def _():
        i = step % nt
        x = x_ref[...]
        ext_ref[0:POOL_HALO, :] = jnp.where(i > 0, halo_ref[...], 0.0)
        ext_ref[POOL_HALO:, :] = x
        gw = x.shape[-1] // len(POOL_WINDOWS)
        pos = lax.broadcasted_iota(jnp.int32, (tm, 1), 0) + i * tm
        ys = []
        for g, w in enumerate(POOL_WINDOWS):
            lo, hi = g * gw, (g + 1) * gw
            xs = x[:, lo:hi]
            s = xs
            for k in range(1, w):
                s = s + ext_ref[POOL_HALO - k:POOL_HALO - k + tm, lo:hi]
            cnt = jnp.minimum(w, pos + 1).astype(F32)
            d = s / cnt - xs
            ys.append(_dot(d.astype(BF16), w_ref[g]))
        y = jnp.concatenate(ys, axis=-1) * scale_ref[...]
        o_ref[...] = _layer_norm(alpha * x + y, g_ref[...], b_ref[...])

    @pl.when(step >= n_main)
    def _():
        o_ref[...] = tail_ref[...]


def _pool_prompt(x, B, T, tail, w_bf, scale, g, b, alpha):
    D = x.shape[1]
    n_tail = tail.shape[0]
    tm = _pick_tile(math.gcd(T, n_tail), 512)
    nt = T // tm
    n_main = B * nt
    hb = tm // POOL_HALO
    G, gw = w_bf.shape[0], w_bf.shape[1]
    main = lambda s: jnp.minimum(s, n_main - 1)
    const2 = lambda s: (0, 0)
    return pl.pallas_call(
        functools.partial(_pool_prompt_kernel, tm=tm, nt=nt, n_main=n_main, alpha=alpha),
        out_shape=jax.ShapeDtypeStruct((B * T + n_tail, D), F32),
        grid=(n_main + n_tail // tm,),
        in_specs=[
            pl.BlockSpec((tm, D), lambda s: (main(s), 0)),
            pl.BlockSpec((POOL_HALO, D), lambda s: (jnp.maximum(main(s) * hb - 1, 0), 0)),
            pl.BlockSpec((tm, D), lambda s: (jnp.maximum(s - n_main, 0), 0)),
            pl.BlockSpec((G, gw, gw), lambda s: (0, 0, 0)),
            pl.BlockSpec((1, D), const2),
            pl.BlockSpec((1, D), const2),
            pl.BlockSpec((1, D), const2),
        ],
        out_specs=pl.BlockSpec((tm, D), lambda s: (s, 0)),
        scratch_shapes=[pltpu.VMEM((tm + POOL_HALO, D), F32)],
        compiler_params=_cparams(("arbitrary",)),
        name="pool_prompt",
    )(x, x, tail, w_bf, scale, g, b)


def _pool_sample_kernel(buf_ref, x_ref, w_ref, scale_ref, g_ref, b_ref, o_ref, *, alpha, pos0):
    nb, ts = buf_ref.shape[0], x_ref.shape[0]
    rows = [buf_ref[r] for r in range(nb)] + [x_ref[t] for t in range(ts)]
    gw = rows[0].shape[-1] // len(POOL_WINDOWS)
    for t in range(ts):
        x = rows[nb + t]
        ys = []
        for g, w in enumerate(POOL_WINDOWS):
            lo, hi = g * gw, (g + 1) * gw
            xs = x[:, lo:hi]
            s = xs
            for k in range(1, w):
                s = s + rows[nb + t - k][:, lo:hi]
            d = s / float(min(w, pos0 + t + 1)) - xs
            ys.append(_dot(d.astype(BF16), w_ref[g]))
        y = jnp.concatenate(ys, axis=-1) * scale_ref[...]
        o_ref[t] = _layer_norm(alpha * x + y, g_ref[...], b_ref[...])


def _pool_sample(buf_t, x_t, w_bf, scale, g, b, alpha, pos0):
    nb, Bs, D = buf_t.shape
    ts = x_t.shape[0]
    bs = _pick_tile(Bs, 32)
    G, gw = w_bf.shape[0], w_bf.shape[1]
    return pl.pallas_call(
        functools.partial(_pool_sample_kernel, alpha=alpha, pos0=pos0),
        out_shape=jax.ShapeDtypeStruct((ts, Bs, D), F32),
        grid=(Bs // bs,),
        in_specs=[
            pl.BlockSpec((nb, bs, D), lambda i: (0, i, 0)),
            pl.BlockSpec((ts, bs, D), lambda i: (0, i, 0)),
            pl.BlockSpec((G, gw, gw), lambda i: (0, 0, 0)),
            pl.BlockSpec((1, D), lambda i: (0, 0)),
            pl.BlockSpec((1, D), lambda i: (0, 0)),
            pl.BlockSpec((1, D), lambda i: (0, 0)),
        ],
        out_specs=pl.BlockSpec((ts, bs, D), lambda i: (0, i, 0)),
        compiler_params=_cparams(("parallel",)),
        name="pool_sample",
    )(buf_t, x_t, w_bf, scale, g, b)


def _ln_ple(x, ff, g_ref, b_ref, p, proj_ref, gw_ref, alpha):
    x2 = _layer_norm(alpha * x + ff, g_ref[...], b_ref[...])
    pp = _dot(p.astype(BF16), proj_ref[...])
    gt = _dot(x2.astype(BF16), gw_ref[...])
    return x2 + pp * _sigmoid(gt)


def _swiglu_partial(xb, wg, wu, wd):
    h = _dot(xb, wg)
    u = _dot(xb, wu)
    return _dot((h * _sigmoid(h) * u).astype(BF16), wd)


def _ffn_kernel(x_ref, wg_ref, wu_ref, wd_ref, g_ref, b_ref, pp_ref, ps_ref, proj_ref, gw_ref, o_ref,
                acc_ref, xb_ref, *, n_main, alpha):
    j = pl.program_id(1)

    @pl.when(j == 0)
    def _():
        acc_ref[...] = jnp.zeros_like(acc_ref)
        xb_ref[...] = x_ref[...].astype(BF16)

    acc_ref[...] += _swiglu_partial(xb_ref[...], wg_ref[...], wu_ref[...], wd_ref[...])

    @pl.when(j == pl.num_programs(1) - 1)
    def _():
        p = jnp.where(pl.program_id(0) < n_main, pp_ref[...], ps_ref[...])
        o_ref[...] = _ln_ple(x_ref[...], acc_ref[...], g_ref, b_ref, p, proj_ref, gw_ref, alpha)


def _ffn_tile(F, target):
    best = LANES
    for t in range(LANES, min(F, target) + 1, LANES):
        if F % t == 0:
            best = t
    return best


def _ple_specs(p_p, p_s, tm):
    n_main = p_p.shape[0] // tm
    head = lambda i, *_: (jnp.minimum(i, n_main - 1), 0)
    tail = lambda i, *_: (jnp.maximum(i - n_main, 0), 0)
    return n_main, [pl.BlockSpec((tm, p_p.shape[1]), head), pl.BlockSpec((tm, p_s.shape[1]), tail)]


def _ffn(x, w_gu, w_down, g, b, p_p, p_s, proj, gate_w, alpha):
    N, D = x.shape
    F = w_down.shape[0]
    P = p_p.shape[1]
    tm = _pick_tile(math.gcd(p_p.shape[0], p_s.shape[0]), 512)
    tf = _ffn_tile(F, 1536)
    nf = F // tf
    n_main, p_specs = _ple_specs(p_p, p_s, tm)
    return pl.pallas_call(
        functools.partial(_ffn_kernel, n_main=n_main, alpha=alpha),
        out_shape=jax.ShapeDtypeStruct((N, D), F32),
        grid=(N // tm, nf),
        in_specs=[
            pl.BlockSpec((tm, D), lambda i, j: (i, 0)),
            pl.BlockSpec((D, tf), lambda i, j: (0, j)),
            pl.BlockSpec((D, tf), lambda i, j: (0, nf + j)),
            pl.BlockSpec((tf, D), lambda i, j: (j, 0)),
            pl.BlockSpec((1, D), lambda i, j: (0, 0)),
            pl.BlockSpec((1, D), lambda i, j: (0, 0)),
            *p_specs,
            pl.BlockSpec((P, D), lambda i, j: (0, 0)),
            pl.BlockSpec((D, D), lambda i, j: (0, 0)),
        ],
        out_specs=pl.BlockSpec((tm, D), lambda i, j: (i, 0)),
        scratch_shapes=[pltpu.VMEM((tm, D), F32), pltpu.VMEM((tm, D), BF16)],
        compiler_params=_cparams(("parallel", "arbitrary")),
        name="ffn_dense",
    )(x, w_gu, w_gu, w_down, g, b, p_p, p_s, proj, gate_w)


def _router_kernel(x_ref, wr_ref, o_ref, rank_ref, run_ref, *, n_e):
    @pl.when(pl.program_id(0) == 0)
    def _():
        run_ref[...] = jnp.zeros_like(run_ref)

    lg = jnp.dot(x_ref[...], wr_ref[...], preferred_element_type=F32, precision=lax.Precision.HIGHEST)
    lane = lax.broadcasted_iota(jnp.int32, lg.shape, 1)
    lg = jnp.where(lane < n_e, lg, -jnp.inf)
    v1 = jnp.max(lg, axis=-1, keepdims=True)
    i1 = jnp.min(jnp.where(lg == v1, lane, LANES), axis=-1, keepdims=True)
    lg2 = jnp.where(lane == i1, -jnp.inf, lg)
    v2 = jnp.max(lg2, axis=-1, keepdims=True)
    i2 = jnp.min(jnp.where(lg2 == v2, lane, LANES), axis=-1, keepdims=True)
    e2 = jnp.exp(v2 - v1)
    den = 1.0 + e2
    gates = jnp.where(lane == i1, 1.0 / den, 0.0) + jnp.where(lane == i2, e2 / den, 0.0)
    o_ref[...] = gates
    tm = gates.shape[0]
    tri = lax.broadcasted_iota(jnp.int32, (tm, tm), 0) >= lax.broadcasted_iota(jnp.int32, (tm, tm), 1)
    sel = jnp.where(gates > 0.0, 1.0, 0.0).astype(BF16)
    incl = run_ref[...] + _dot(jnp.where(tri, 1.0, 0.0).astype(BF16), sel)
    rank_ref[...] = incl
    run_ref[...] = incl[tm - 1:tm, :]


def _router(x, wr_pad, n_e):
    N, D = x.shape
    tm = _pick_tile(N, 512)
    return pl.pallas_call(
        functools.partial(_router_kernel, n_e=n_e),
        out_shape=(jax.ShapeDtypeStruct((N, LANES), F32), jax.ShapeDtypeStruct((N, LANES), F32)),
        grid=(N // tm,),
        in_specs=[pl.BlockSpec((tm, D), lambda i: (i, 0)),
                  pl.BlockSpec((D, LANES), lambda i: (0, 0))],
        out_specs=(pl.BlockSpec((tm, LANES), lambda i: (i, 0)), pl.BlockSpec((tm, LANES), lambda i: (i, 0))),
        scratch_shapes=[pltpu.VMEM((1, LANES), F32)],
        compiler_params=_cparams(("arbitrary",)),
        name="router",
    )(x, wr_pad)


MOE_ROW_TILE = 1024
MOE_SUB_TILE = 512


def _route_plan(gates, incl, n_e, tm):
    N = gates.shape[0]
    n_tiles = -(-TOP_K * N // tm) + n_e
    sel = gates[:, :n_e] > 0.0
    incl = incl[:, :n_e].astype(jnp.int32)
    cnt = incl[-1]
    padded = (cnt + tm - 1) // tm * tm
    ends = jnp.cumsum(padded)
    pos = (ends - padded)[None, :] + incl - 1
    rows = n_tiles * tm
    pos_lo = jnp.minimum(jnp.min(jnp.where(sel, pos, rows), axis=1), rows - 1)
    pos_hi = jnp.max(jnp.where(sel, pos, -1), axis=1)
    pos_hi = jnp.where(pos_hi < 0, pos_lo, pos_hi)
    tile_start = jnp.arange(n_tiles, dtype=jnp.int32) * tm
    tile_e = jnp.minimum(jnp.sum(tile_start[:, None] >= ends[None, :], axis=1), n_e - 1).astype(jnp.int32)
    used = jnp.clip(jnp.take(ends - padded + cnt, tile_e) - tile_start, 0, tm)
    used = jnp.where(tile_start < ends[-1], used, 0).astype(jnp.int32)
    last_e = jnp.max(jnp.where(used > 0, tile_e, 0))
    tile_e = jnp.where(tile_start < ends[-1], tile_e, last_e)
    tails = jnp.concatenate([ends - padded + cnt, ends[-1:] // tm]).astype(jnp.int32)
    return pos_lo.astype(jnp.int32), pos_hi.astype(jnp.int32), tile_e, used, tails, rows


def _row_copy(src_ref, src_row, dst_ref, dst_row, sem):
    return pltpu.make_async_copy(src_ref.at[pl.ds(src_row, 1)], dst_ref.at[pl.ds(dst_row, 1)], sem)


def _dispatch_kernel(lo_ref, hi_ref, tails_ref, x_ref, xs_ref, zero_ref, sem, *, tm, n_e):
    base = pl.program_id(0) * tm

    @pl.when(pl.program_id(0) == 0)
    def _():
        zero_ref[...] = jnp.zeros_like(zero_ref)
        fills = []
        for e in range(n_e):
            first = pl.multiple_of(tails_ref[e] // SUBLANES * SUBLANES, SUBLANES)
            fills.append(pltpu.make_async_copy(zero_ref, xs_ref.at[pl.ds(first, zero_ref.shape[0])], sem))
        for f in fills:
            f.start()
        for f in fills:
            f.wait()

        n_fill = zero_ref.shape[0]
        row_tile = n_fill - SUBLANES
        n_tiles = (xs_ref.shape[0] - n_fill) // row_tile

        def zero_tile(t, c):
            start = pl.multiple_of(t * row_tile, row_tile)
            blank = pltpu.make_async_copy(zero_ref.at[pl.ds(0, row_tile)], xs_ref.at[pl.ds(start, row_tile)], sem)
            blank.start()
            blank.wait()
            return c

        lax.fori_loop(tails_ref[n_e], n_tiles, zero_tile, 0)
        spare = pltpu.make_async_copy(zero_ref, xs_ref.at[pl.ds(n_tiles * row_tile, n_fill)], sem)
        spare.start()
        spare.wait()

    def start(r, c):
        _row_copy(x_ref, r, xs_ref, lo_ref[base + r], sem).start()
        _row_copy(x_ref, r, xs_ref, hi_ref[base + r], sem).start()
        return c

    lax.fori_loop(0, tm, start, 0)
    for _ in range(TOP_K):
        pltpu.make_async_copy(x_ref, xs_ref.at[pl.ds(0, tm)], sem).wait()


def _dispatch(pos_lo, pos_hi, tails, x, rows, row_tile):
    N, D = x.shape
    tm = _pick_tile(N, 256)
    fill = row_tile + SUBLANES
    return pl.pallas_call(
        functools.partial(_dispatch_kernel, tm=tm, n_e=tails.shape[0] - 1),
        out_shape=jax.ShapeDtypeStruct((rows + fill, D), F32),
        grid_spec=pltpu.PrefetchScalarGridSpec(
            num_scalar_prefetch=3,
            grid=(N // tm,),
            in_specs=[pl.BlockSpec((tm, D), lambda i, lo, hi, tl: (i, 0))],
            out_specs=pl.BlockSpec(memory_space=pl.ANY),
            scratch_shapes=[pltpu.VMEM((fill, D), F32), pltpu.SemaphoreType.DMA],
        ),
        compiler_params=_cparams(("arbitrary",)),
        name="moe_dispatch",
    )(pos_lo, pos_hi, tails, x)


def _experts_kernel(te_ref, used_ref, x_ref, wg_ref, wu_ref, wd_ref, o_ref, acc_ref, xb_ref, *, sub):
    r, j = pl.program_id(0), pl.program_id(1)
    used = used_ref[r]
    tm = x_ref.shape[0]
    n_sub = tm // sub

    @pl.when(j == 0)
    def _():
        acc_ref[...] = jnp.zeros_like(acc_ref)

    @pl.when((j == 0) & (used > 0))
    def _():
        xb_ref[...] = x_ref[...].astype(BF16)

    @pl.when(used > tm - sub)
    def _():
        acc_ref[...] += _swiglu_partial(xb_ref[...], wg_ref[...], wu_ref[...], wd_ref[...])

    for s in range(n_sub - 1):
        @pl.when((used > s * sub) & (used <= tm - sub))
        def _():
            rs = pl.ds(s * sub, sub)
            acc_ref[rs, :] += _swiglu_partial(xb_ref[rs, :], wg_ref[...], wu_ref[...], wd_ref[...])

    @pl.when(j == pl.num_programs(1) - 1)
    def _():
        o_ref[...] = acc_ref[...]


def _experts(tile_e, used, xs, w_gu, w_down, layer, tm):
    D = xs.shape[1]
    rows = tile_e.shape[0] * tm
    F = w_down.shape[2]
    tf = _ffn_tile(F, 512)
    nf = F // tf

    def jmap(r, j, used_ref):
        return jnp.where(used_ref[r] > 0, j, nf - 1)

    return pl.pallas_call(
        functools.partial(_experts_kernel, sub=min(MOE_SUB_TILE, tm)),
        out_shape=jax.ShapeDtypeStruct((rows, D), F32),
        grid_spec=pltpu.PrefetchScalarGridSpec(
            num_scalar_prefetch=2,
            grid=(rows // tm, nf),
            in_specs=[
                pl.BlockSpec((tm, D), lambda r, j, te, us: (r, 0)),
                pl.BlockSpec((None, None, D, tf), lambda r, j, te, us: (layer, te[r], 0, jmap(r, j, us))),
                pl.BlockSpec((None, None, D, tf), lambda r, j, te, us: (layer, te[r], 0, nf + jmap(r, j, us))),
                pl.BlockSpec((None, None, tf, D), lambda r, j, te, us: (layer, te[r], jmap(r, j, us), 0)),
            ],
            out_specs=pl.BlockSpec((tm, D), lambda r, j, te, us: (r, 0)),
            scratch_shapes=[pltpu.VMEM((tm, D), F32), pltpu.VMEM((tm, D), BF16)],
        ),
        compiler_params=_cparams(("parallel", "arbitrary")),
        name="moe_experts",
    )(tile_e, used, xs, w_gu, w_gu, w_down)


def _combine_kernel(lo_ref, hi_ref, x_ref, gates_ref, ys_ref, g_ref, b_ref, pp_ref, ps_ref, proj_ref, gw_ref,
                    *refs, tm, n_main, alpha):
    *o_refs, ya_ref, yb_ref, sems = refs
    step = pl.program_id(0)
    slot = step % 2

    last = pl.num_programs(0) - 1

    def start(tile, dst_slot, r):
        _row_copy(ys_ref, lo_ref[tile * tm + r], ya_ref.at[dst_slot], r, sems.at[dst_slot]).start()
        _row_copy(ys_ref, hi_ref[tile * tm + r], yb_ref.at[dst_slot], r, sems.at[dst_slot]).start()

    def wait(w_slot):
        pltpu.make_async_copy(ys_ref.at[pl.ds(0, tm)], ya_ref.at[w_slot], sems.at[w_slot]).wait()
        pltpu.make_async_copy(ys_ref.at[pl.ds(0, tm)], yb_ref.at[w_slot], sems.at[w_slot]).wait()

    @pl.when(step == 0)
    def _():
        lax.fori_loop(0, tm, lambda r, c: (start(step, slot, r), c)[1], 0)

    nxt = jnp.minimum(step + 1, last)
    for r in range(tm):
        start(nxt, 1 - slot, r)

    gates = gates_ref[...]
    lane = lax.broadcasted_iota(jnp.int32, gates.shape, 1)
    sel = gates > 0.0
    e_lo = jnp.min(jnp.where(sel, lane, LANES), axis=-1, keepdims=True)
    e_hi = jnp.max(jnp.where(sel, lane, -1), axis=-1, keepdims=True)
    g_lo = jnp.sum(jnp.where(lane == e_lo, gates, 0.0), axis=-1, keepdims=True)
    g_hi = jnp.where(e_hi != e_lo, jnp.sum(jnp.where(lane == e_hi, gates, 0.0), axis=-1, keepdims=True), 0.0)
    wait(slot)
    ff = g_lo * ya_ref[slot] + g_hi * yb_ref[slot]
    p = jnp.where(step < n_main, pp_ref[...], ps_ref[...])
    out = _ln_ple(x_ref[...], ff, g_ref, b_ref, p, proj_ref, gw_ref, alpha)
    if len(o_refs) == 1:
        o_refs[0][...] = out
    else:
        @pl.when(step < n_main)
        def _():
            o_refs[0][...] = out

        @pl.when(step >= n_main)
        def _():
            o_refs[1][...] = out

    @pl.when(step == last)
    def _():
        wait(1 - slot)


def _combine(pos_lo, pos_hi, x, gates, ys, g, b, p_p, p_s, proj, gate_w, alpha, split):
    N, D = x.shape
    P = p_p.shape[1]
    tm = _pick_tile(math.gcd(p_p.shape[0], p_s.shape[0]), 256)
    row = lambda i, lo, hi: (i, 0)
    const = lambda i, lo, hi: (0, 0)
    n_main, p_specs = _ple_specs(p_p, p_s, tm)
    if split:
        n_p = n_main * tm
        out_shape = (jax.ShapeDtypeStruct((n_p, D), F32), jax.ShapeDtypeStruct((N - n_p, D), F32))
        out_specs = (pl.BlockSpec((tm, D), lambda i, lo, hi: (jnp.minimum(i, n_main - 1), 0)),
                     pl.BlockSpec((tm, D), lambda i, lo, hi: (jnp.maximum(i - n_main, 0), 0)))
    else:
        out_shape = jax.ShapeDtypeStruct((N, D), F32)
        out_specs = pl.BlockSpec((tm, D), row)
    return pl.pallas_call(
        functools.partial(_combine_kernel, tm=tm, n_main=n_main, alpha=alpha),
        out_shape=out_shape,
        grid_spec=pltpu.PrefetchScalarGridSpec(
            num_scalar_prefetch=2,
            grid=(N // tm,),
            in_specs=[pl.BlockSpec((tm, D), row),
                      pl.BlockSpec((tm, LANES), row),
                      pl.BlockSpec(memory_space=pl.ANY),
                      pl.BlockSpec((1, D), const),
                      pl.BlockSpec((1, D), const),
                      *p_specs,
                      pl.BlockSpec((P, D), const),
                      pl.BlockSpec((D, D), const)],
            out_specs=out_specs,
            scratch_shapes=[pltpu.VMEM((2, tm, D), F32), pltpu.VMEM((2, tm, D), F32),
                            pltpu.SemaphoreType.DMA((2,))],
        ),
        compiler_params=_cparams(("arbitrary",)),
        name="moe_combine",
    )(pos_lo, pos_hi, x, gates, ys, g, b, p_p, p_s, proj, gate_w)


def _moe(x, wr_pad, n_e, w_gu, w_down, layer, g, b, p_p, p_s, proj, gate_w, alpha, split=False):
    gates, incl = _router(x, wr_pad, n_e)
    tm = MOE_ROW_TILE
    pos_lo, pos_hi, tile_e, used, tails, rows = _route_plan(gates, incl, n_e, tm)
    xs = _dispatch(pos_lo, pos_hi, tails, x, rows, tm)
    ys = _experts(tile_e, used, xs, w_gu, w_down, layer, tm)
    return _combine(pos_lo, pos_hi, x, gates, ys, g, b, p_p, p_s, proj, gate_w, alpha, split)


def _kv_kernel(x_ref, g_ref, b_ref, w_ref, wvt_ref, kp_ref, vp_ref, ks_ref, vs_ref, kb_ref, vb_ref, vt_ref,
               *, n_main):
    xn = _layer_norm(x_ref[...], g_ref[...], b_ref[...]).astype(BF16)
    qk = kp_ref.shape[-1]
    k = _dot(xn, w_ref[:, :qk])
    v = _dot(xn, w_ref[:, qk:])

    @pl.when(pl.program_id(0) < n_main)
    def _():
        kp_ref[...] = k
        vp_ref[...] = v

    @pl.when(pl.program_id(0) >= n_main)
    def _():
        ks_ref[...] = k
        vs_ref[...] = v

    kb_ref[...] = k.astype(BF16)
    vb_ref[...] = v.astype(BF16)
    vt_ref[...] = _dot_nt(wvt_ref[...], xn).astype(BF16)


def _shared_kv(x, n_p, g, b, w_kv_bf, w_vt_bf, qk_width):
    N, D = x.shape
    n_s = N - n_p
    W = w_kv_bf.shape[1]
    vw = W - qk_width
    tm = _pick_tile(math.gcd(n_p, n_s), 512)
    n_main = n_p // tm
    row = lambda i: (i, 0)
    const = lambda i: (0, 0)
    head = lambda i: (jnp.minimum(i, n_main - 1), 0)
    tail = lambda i: (jnp.maximum(i - n_main, 0), 0)
    return pl.pallas_call(
        functools.partial(_kv_kernel, n_main=n_main),
        out_shape=(jax.ShapeDtypeStruct((n_p, qk_width), F32), jax.ShapeDtypeStruct((n_p, vw), F32),
                   jax.ShapeDtypeStruct((n_s, qk_width), F32), jax.ShapeDtypeStruct((n_s, vw), F32),
                   jax.ShapeDtypeStruct((N, qk_width), BF16), jax.ShapeDtypeStruct((N, vw), BF16),
                   jax.ShapeDtypeStruct((vw, N), BF16)),
        grid=(N // tm,),
        in_specs=[pl.BlockSpec((tm, D), row), pl.BlockSpec((1, D), const), pl.BlockSpec((1, D), const),
                  pl.BlockSpec((D, W), const), pl.BlockSpec((vw, D), const)],
        out_specs=(pl.BlockSpec((tm, qk_width), head), pl.BlockSpec((tm, vw), head),
                   pl.BlockSpec((tm, qk_width), tail), pl.BlockSpec((tm, vw), tail),
                   pl.BlockSpec((tm, qk_width), row), pl.BlockSpec((tm, vw), row),
                   pl.BlockSpec((vw, tm), lambda i: (0, i))),
        compiler_params=_cparams(("arbitrary",)),
        name="shared_kv",
    )(x, g, b, w_kv_bf, w_vt_bf)


def _proj_kernel(x_ref, w_ref, o_ref, *, scale):
    o_ref[...] = (_dot(x_ref[...].astype(BF16), w_ref[...]) * scale).astype(o_ref.dtype)


def _proj(x, w_bf, scale):
    N, D = x.shape
    W = w_bf.shape[1]
    tm = _pick_tile(N, 512)
    return pl.pallas_call(
        functools.partial(_proj_kernel, scale=scale),
        out_shape=jax.ShapeDtypeStruct((N, W), BF16),
        grid=(N // tm,),
        in_specs=[pl.BlockSpec((tm, D), lambda i: (i, 0)),
                  pl.BlockSpec((D, W), lambda i: (0, 0))],
        out_specs=pl.BlockSpec((tm, W), lambda i: (i, 0)),
        compiler_params=_cparams(("parallel",)),
        name="q_proj",
    )(x, w_bf)


def _out_ln_kernel(op_ref, os_ref, w_ref, x_ref, g_ref, b_ref, y_ref, *, n_main, alpha):
    o = jnp.where(pl.program_id(0) < n_main, op_ref[...], os_ref[...])
    y_ref[...] = _layer_norm(alpha * x_ref[...] + _dot(o, w_ref[...]), g_ref[...], b_ref[...])


def _out_ln(o_p, o_s, w_bf, x, g, b, alpha):
    N, D = x.shape
    W = o_p.shape[1]
    tm = _pick_tile(math.gcd(o_p.shape[0], o_s.shape[0]), 512)
    n_main = o_p.shape[0] // tm
    return pl.pallas_call(
        functools.partial(_out_ln_kernel, n_main=n_main, alpha=alpha),
        out_shape=jax.ShapeDtypeStruct((N, D), F32),
        grid=(N // tm,),
        in_specs=[pl.BlockSpec((tm, W), lambda i: (jnp.minimum(i, n_main - 1), 0)),
                  pl.BlockSpec((tm, W), lambda i: (jnp.maximum(i - n_main, 0), 0)),
                  pl.BlockSpec((W, D), lambda i: (0, 0)),
                  pl.BlockSpec((tm, D), lambda i: (i, 0)),
                  pl.BlockSpec((1, D), lambda i: (0, 0)),
                  pl.BlockSpec((1, D), lambda i: (0, 0))],
        out_specs=pl.BlockSpec((tm, D), lambda i: (i, 0)),
        compiler_params=_cparams(("parallel",)),
        name="attn_out_ln",
    )(o_p, o_s, w_bf, x, g, b)


def _lambda_value(lam_ref, lam_init):
    lq = lam_ref[...]
    s1 = jnp.sum(lq[0:1] * lq[1:2], axis=-1, keepdims=True)
    s2 = jnp.sum(lq[2:3] * lq[3:4], axis=-1, keepdims=True)
    return jnp.exp(s1) - jnp.exp(s2) + lam_init


def _head_norm(o, g, lam_init):
    return o * lax.rsqrt(jnp.mean(o * o, axis=-1, keepdims=True) + RMS_EPS) * g * (1.0 - lam_init)


def _softmax_update(s, v, m_ref, l_ref, acc_ref, idx):
    m_old = m_ref[idx]
    m_new = jnp.maximum(m_old, jnp.max(s, axis=-1, keepdims=True))
    a = jnp.exp(m_old - m_new)
    p = jnp.exp(s - m_new)
    l_ref[idx] = a * l_ref[idx] + jnp.sum(p, axis=-1, keepdims=True)
    acc_ref[idx] = a * acc_ref[idx] + _dot(p.astype(BF16), v)
    m_ref[idx] = m_new


KPOS_RADIX = 256
PROMPT_HEADS_PER_STEP = 8
STAT_ROWS = 8
SUM_ROWS = 16


def _attn_prompt_kernel(ti_ref, tj_ref, slopes_ref, lam_ref, q_ref, k_ref, vt_ref, g_ref, o_ref,
                        qa_ref, m_ref, acc_ref, *, tq, hd, hp, lam_init):
    hg, pair = pl.program_id(1), pl.program_id(2)
    i, j = ti_ref[pair], tj_ref[pair]
    vd = 2 * hd

    @pl.when(j == 0)
    def _():
        m_ref[...] = jnp.full_like(m_ref, NEG_INF)
        acc_ref[...] = jnp.zeros_like(acc_ref)
        lane = lax.broadcasted_iota(jnp.int32, (tq, vd), 1)
        for hh in range(hp):
            q = q_ref[:, hh * vd:(hh + 1) * vd]
            slope = slopes_ref[hg * hp + hh]
            aug = jnp.where(lane == 0, slope * KPOS_RADIX, jnp.where(lane == 1, slope, 0.0)).astype(BF16)
            qa_ref[hh, 0:tq, 0:vd] = jnp.where(lane < hd, q, jnp.zeros_like(q))
            qa_ref[hh, tq:, 0:vd] = jnp.where(lane >= hd, q, jnp.zeros_like(q))
            qa_ref[hh, 0:tq, vd:] = aug
            qa_ref[hh, tq:, vd:] = aug

    def block(diagonal):
        kpos = lax.broadcasted_iota(jnp.int32, (tq, vd), 0) + j * tq
        lane = lax.broadcasted_iota(jnp.int32, (tq, vd), 1)
        kaug = jnp.where(lane == 0, kpos // KPOS_RADIX, jnp.where(lane == 1, kpos % KPOS_RADIX, 0))
        kaug = kaug.astype(F32).astype(BF16)
        ones = jnp.ones((SUM_ROWS, tq), BF16)
        if diagonal:
            kr = lax.broadcasted_iota(jnp.int32, (tq, 2 * tq), 0)
            qc = lax.broadcasted_iota(jnp.int32, (tq, 2 * tq), 1)
            visible = jnp.where(qc >= tq, qc - tq, qc) >= kr

        def scores(hh):
            ka = jnp.concatenate([k_ref[:, hh * vd:(hh + 1) * vd], kaug], axis=1)
            return _dot_nt(ka, qa_ref[hh])

        s_next = scores(0)
        for hh in range(hp):
            s = s_next
            if hh + 1 < hp:
                s_next = scores(hh + 1)
            if diagonal:
                s = jnp.where(visible, s, NEG_INF)
            m_old = m_ref[hh]
            m_new = jnp.maximum(m_old, jnp.max(s, axis=0, keepdims=True))
            a = jnp.exp(m_old[0:1] - m_new[0:1])
            p = jnp.exp(s - m_new[0:1]).astype(BF16)
            vt1 = jnp.concatenate([vt_ref[hh * vd:(hh + 1) * vd, :], ones], axis=0)
            acc_ref[hh] = a * acc_ref[hh] + _dot(vt1, p)
            m_ref[hh] = m_new

    @pl.when(j < i)
    def _():
        block(False)

    @pl.when(j == i)
    def _():
        block(True)
        lam = _lambda_value(lam_ref, lam_init)
        for hh in range(hp):
            acc = acc_ref[hh]
            ot = acc[:vd] / acc[vd:vd + 1]
            ot = ot[:, :tq] - lam * ot[:, tq:]
            ot = ot * lax.rsqrt(jnp.mean(ot * ot, axis=0, keepdims=True) + RMS_EPS)
            o = ot.T * g_ref[:, hh * vd:(hh + 1) * vd] * (1.0 - lam_init)
            o_ref[:, hh * vd:(hh + 1) * vd] = o.astype(o_ref.dtype)


def _attn_prompt(q, k, vt, slopes, lam_qk, subln_g, B, T, H, lam_init, hp):
    W = q.shape[1]
    vd = W // H
    hd = vd // 2
    tq = _pick_tile(T, 512)
    nq = T // tq
    assert T <= KPOS_RADIX * KPOS_RADIX and H % hp == 0
    pairs = [(i, j) for i in range(nq) for j in range(i + 1)]
    ti = jnp.asarray([p[0] for p in pairs], jnp.int32)
    tj = jnp.asarray([p[1] for p in pairs], jnp.int32)
    return pl.pallas_call(
        functools.partial(_attn_prompt_kernel, tq=tq, hd=hd, hp=hp, lam_init=lam_init),
        out_shape=jax.ShapeDtypeStruct((B * T, W), BF16),
        grid_spec=pltpu.PrefetchScalarGridSpec(
            num_scalar_prefetch=2,
            grid=(B, H // hp, len(pairs)),
            in_specs=[
                pl.BlockSpec(memory_space=pltpu.SMEM),
                pl.BlockSpec(lam_qk.shape, lambda b, h, p, ti, tj: (0, 0)),
                pl.BlockSpec((tq, hp * vd), lambda b, h, p, ti, tj: (b * nq + ti[p], h)),
                pl.BlockSpec((tq, hp * vd), lambda b, h, p, ti, tj: (b * nq + tj[p], h)),
                pl.BlockSpec((hp * vd, tq), lambda b, h, p, ti, tj: (h, b * nq + tj[p])),
                pl.BlockSpec((1, hp * vd), lambda b, h, p, ti, tj: (0, h)),
            ],
            out_specs=pl.BlockSpec((tq, hp * vd), lambda b, h, p, ti, tj: (b * nq + ti[p], h)),
            scratch_shapes=[pltpu.VMEM((hp, 2 * tq, 2 * vd), BF16), pltpu.VMEM((hp, STAT_ROWS, 2 * tq), F32),
                            pltpu.VMEM((hp, vd + SUM_ROWS, 2 * tq), F32)],
        ),
        compiler_params=_cparams(("parallel", "parallel", "arbitrary")),
        name="attn_prompt",
    )(ti, tj, slopes, lam_qk, q, k, vt, subln_g)


SAMPLE_PAGES_PER_STEP = 8


def _attn_sample_kernel(pt_ref, lam_ref, qz_ref, *refs, gp, n_steps, lam_init):
    k_refs, v_refs = refs[:gp], refs[gp:2 * gp]
    kn_ref, vn_ref, bias_ref, biasn_ref, g_ref, o_ref, m_ref, l_ref, acc_ref = refs[2 * gp:]
    st = pl.program_id(1)
    rows = k_refs[0].shape[1]

    @pl.when(st == 0)
    def _():
        m_ref[...] = jnp.full_like(m_ref, NEG_INF)
        l_ref[...] = jnp.zeros_like(l_ref)
        acc_ref[...] = jnp.zeros_like(acc_ref)

    qz = qz_ref[0]
    for step in range(n_steps):
        @pl.when(st == step)
        def _():
            ss = []
            for g in range(gp):
                pg = step * gp + g
                ss.append(_dot_nt(qz, k_refs[g][0].astype(BF16)) + bias_ref[:, pg * rows:(pg + 1) * rows])
            vb = jnp.concatenate([v_refs[g][0].astype(BF16) for g in range(gp)], axis=0)
            _softmax_update(jnp.concatenate(ss, axis=1), vb, m_ref, l_ref, acc_ref, 0)

    @pl.when(st == n_steps - 1)
    def _():
        _softmax_update(_dot_nt(qz, kn_ref[0]) + biasn_ref[...], vn_ref[0], m_ref, l_ref, acc_ref, 0)
        lam = _lambda_value(lam_ref, lam_init)
        o = acc_ref[0] / l_ref[0]
        half = o.shape[0] // 2
        o = o[:half] - lam * o[half:]
        o_ref[0] = _head_norm(o, g_ref[...], lam_init).astype(o_ref.dtype)


def _attn_sample(page_table, qz, cache_k2, cache_v2, k_new, v_new, bias, bias_new, g_rows, lam_qk, lam_init):
    Bs, R, vd = qz.shape
    n_pages = page_table.shape[1]
    gp = math.gcd(n_pages, SAMPLE_PAGES_PER_STEP)
    n_steps = n_pages // gp
    rows = cache_k2.shape[1]
    rn = k_new.shape[1]
    pt = page_table.reshape(-1)

    def page_spec(g):
        return pl.BlockSpec((1, rows, vd), lambda b, s, pt_ref: (pt_ref[b * n_pages + s * gp + g], 0, 0))

    const2 = lambda b, s, pt_ref: (0, 0)
    seq3 = lambda b, s, pt_ref: (b, 0, 0)
    return pl.pallas_call(
        functools.partial(_attn_sample_kernel, gp=gp, n_steps=n_steps, lam_init=lam_init),
        out_shape=jax.ShapeDtypeStruct((Bs, R // 2, vd), BF16),
        grid_spec=pltpu.PrefetchScalarGridSpec(
            num_scalar_prefetch=1,
            grid=(Bs, n_steps),
            in_specs=[pl.BlockSpec(lam_qk.shape, const2), pl.BlockSpec((1, R, vd), seq3)]
            + [page_spec(g) for g in range(gp)] + [page_spec(g) for g in range(gp)]
            + [pl.BlockSpec((1, rn, vd), seq3), pl.BlockSpec((1, rn, vd), seq3),
               pl.BlockSpec(bias.shape, const2), pl.BlockSpec(bias_new.shape, const2),
               pl.BlockSpec(g_rows.shape, const2)],
            out_specs=pl.BlockSpec((1, R // 2, vd), seq3),
            scratch_shapes=[pltpu.VMEM((1, R, 1), F32), pltpu.VMEM((1, R, 1), F32),
                            pltpu.VMEM((1, R, vd), F32)],
        ),
        compiler_params=_cparams(("parallel", "arbitrary")),
        name="attn_sample",
    )(pt, lam_qk, qz, *([cache_k2] * gp), *([cache_v2] * gp), k_new, v_new, bias, bias_new, g_rows)


def _alibi_slopes(H):
    return 2.0 ** (-8.0 * np.arange(1, H + 1, dtype=np.float64) / H)


def _sample_bias_tables(H, Ts, page, n_pages):
    slopes = _alibi_slopes(H)
    pos0 = n_pages * page
    hq = np.tile(np.repeat(np.arange(H), Ts), 2)
    tq = np.tile(np.arange(Ts), 2 * H)
    key = np.repeat(np.arange(n_pages * page), H)
    hk = np.tile(np.arange(H), n_pages * page)
    dist = (pos0 + tq)[:, None] - key[None, :]
    bias = np.where(hq[:, None] == hk[None, :], -slopes[hq][:, None] * dist, NEG_INF)
    tn = np.repeat(np.arange(Ts), H)
    hn = np.tile(np.arange(H), Ts)
    dn = tq[:, None] - tn[None, :]
    bias_new = np.where((hq[:, None] == hn[None, :]) & (dn >= 0), -slopes[hq][:, None] * dn, NEG_INF)
    return jnp.asarray(bias, F32), jnp.asarray(bias_new, F32)


def kernel(x_prompt, x_sample, state_pool, cache_k, cache_v, page_table, p_prompt, p_sample, pool_w, pool_scale, ln1_g, ln1_b, ln2_g, ln2_b, ln_kv_g, ln_kv_b, w_kv, w_q, lam_qk, subln_g, w_o, dense_gu, dense_down, w_router, moe_gu, moe_down, ple_proj, ple_gate):
    B, T, D = x_prompt.shape
    Bs, Ts, _ = x_sample.shape
    depth = ln1_g.shape[0]
    n_a = pool_w.shape[0]
    n_pool, page, H, vd = cache_v.shape
    hd = vd // 2
    qk_width = H * 2 * hd
    n_e = w_router.shape[-1]
    n_pages = page_table.shape[1]
    pos0 = n_pages * page
    n_p, n_s = B * T, Bs * Ts
    n_all = n_p + n_s
    alpha = (2 * depth) ** 0.25
    row = lambda a: a.reshape(1, -1)

    bf = lambda w: w.astype(BF16)
    moe_gu_bf, moe_down_bf = bf(moe_gu), bf(moe_down)
    wr_pad = jnp.pad(w_router, ((0, 0), (0, 0), (0, LANES - n_e)))

    p_p, p_s = p_prompt.reshape(depth, n_p, -1), p_sample.reshape(depth, n_s, -1)
    slopes_np = _alibi_slopes(H)
    assert np.all((slopes_np * KPOS_RADIX).astype(BF16).astype(np.float64) == slopes_np * KPOS_RADIX)
    assert np.all(slopes_np.astype(BF16).astype(np.float64) == slopes_np)
    slopes = jnp.asarray(slopes_np, F32)
    bias, bias_new = _sample_bias_tables(H, Ts, page, n_pages)

    x_rows = x_prompt.reshape(n_p, D)
    x_s = x_sample
    x_all = None
    pool_p, pool_s = [], []
    kv = None
    for i in range(depth):
        if i < n_a:
            pool_p.append(jnp.stack([x_rows[(bb + 1) * T - POOL_BUF:(bb + 1) * T] for bb in range(B)]))
            pool_s.append(jnp.concatenate([state_pool[i][:, Ts:], x_s], axis=1))
            pw = bf(pool_w[i])
            x1_s = _pool_sample(state_pool[i].transpose(1, 0, 2), x_s.transpose(1, 0, 2), pw,
                                row(pool_scale[i]), row(ln1_g[i]), row(ln1_b[i]), alpha, pos0)
            x1 = _pool_prompt(x_rows, B, T, x1_s.transpose(1, 0, 2).reshape(n_s, D), pw, row(pool_scale[i]),
                              row(ln1_g[i]), row(ln1_b[i]), alpha)
        else:
            jj = i - n_a
            lam_init = 0.8 - 0.6 * math.exp(-0.3 * i)
            if kv is None:
                kv = _shared_kv(x_all, n_p, row(ln_kv_g), row(ln_kv_b), bf(w_kv), bf(w_kv[:, qk_width:].T),
                                qk_width)
            k_b, v_b, v_t = kv[4:]
            q = _proj(x_all, bf(w_q[jj]), hd ** -0.5)
            o = _attn_prompt(q, k_b, v_t, slopes, lam_qk[jj], row(subln_g[jj]), B, T, H, lam_init,
                             math.gcd(H, PROMPT_HEADS_PER_STEP))
            q_s = q[n_p:].reshape(Bs, Ts, H, vd).transpose(0, 2, 1, 3).reshape(Bs, H * Ts, vd)
            lane = jnp.arange(vd)
            qz = jnp.concatenate([jnp.where(lane < hd, q_s, 0), jnp.where(lane >= hd, q_s, 0)], axis=1)
            g_rows = jnp.repeat(subln_g[jj].reshape(H, vd), Ts, axis=0)
            o_s = _attn_sample(page_table, qz, cache_k.reshape(n_pool, page * H, 2 * hd),
                               cache_v.reshape(n_pool, page * H, vd),
                               k_b[n_p:].reshape(Bs, Ts * H, 2 * hd), v_b[n_p:].reshape(Bs, Ts * H, vd),
                               bias, bias_new, g_rows, lam_qk[jj], lam_init)
            o_s = o_s.reshape(Bs, H, Ts, vd).transpose(0, 2, 1, 3).reshape(n_s, H * vd)
            x1 = _out_ln(o, o_s, bf(w_o[jj]), x_all, row(ln1_g[i]), row(ln1_b[i]), alpha)
        tail_args = (row(ln2_g[i]), row(ln2_b[i]), p_p[i], p_s[i], bf(ple_proj[i]), bf(ple_gate[i]), alpha)
        last_moe = i % 2 == 1 and i + 1 == depth
        if i % 2 == 0:
            x_all = _ffn(x1, bf(dense_gu[i // 2]), bf(dense_down[i // 2]), *tail_args)
        elif not last_moe:
            x_all = _moe(x1, wr_pad[i // 2], n_e, moe_gu_bf, moe_down_bf, i // 2, *tail_args)
        else:
            y_p, y_s = _moe(x1, wr_pad[i // 2], n_e, moe_gu_bf, moe_down_bf, i // 2, *tail_args, split=True)
        if not last_moe:
            x_rows = x_all
            x_s = x_all[n_p:].reshape(Bs, Ts, D)
            y_p, y_s = x_all[:n_p], x_all[n_p:]

    k_p, v_p, k_s, v_s = kv[:4]
    return (y_p.reshape(B, T, D), y_s.reshape(Bs, Ts, D), jnp.stack(pool_p), jnp.stack(pool_s),
            k_p.reshape(B, T, H, 2 * hd), v_p.reshape(B, T, H, vd),
            k_s.reshape(Bs, Ts, H, 2 * hd), v_s.reshape(Bs, Ts, H, vd))
```

```python
import functools
import math

import numpy as np
import jax
import jax.numpy as jnp
from jax import lax
from jax.experimental import pallas as pl
from jax.experimental.pallas import tpu as pltpu

F32 = jnp.float32
BF16 = jnp.bfloat16

POOL_WINDOWS = (2, 4, 8, 16)
POOL_BUF = max(POOL_WINDOWS) - 1
POOL_HALO = 16
TOP_K = 2
LN_EPS = 1e-5
RMS_EPS = 1e-5
NEG_INF = -1e30
LANES = 128
SUBLANES = 8
VMEM_LIMIT = 56 * 1024 * 1024


def _cparams(sem):
    return pltpu.CompilerParams(dimension_semantics=sem, vmem_limit_bytes=VMEM_LIMIT)


def _layer_norm(x, g, b):
    mu = jnp.mean(x, axis=-1, keepdims=True)
    xc = x - mu
    var = jnp.mean(xc * xc, axis=-1, keepdims=True)
    return xc * lax.rsqrt(var + LN_EPS) * g + b


def _sigmoid(x):
    return 1.0 / (1.0 + jnp.exp(-x))


def _dot(a, b):
    return jnp.dot(a, b, preferred_element_type=F32)


def _dot_nt(a, b):
    return lax.dot_general(a, b, (((1,), (1,)), ((), ())), preferred_element_type=F32)


def _pick_tile(n, target):
    t = min(n, target)
    while n % t:
        t //= 2
    return t


def _pool_prompt_kernel(x_ref, halo_ref, tail_ref, w_ref, scale_ref, g_ref, b_ref, o_ref, ext_ref,
                        *, tm, nt, n_main, alpha):
    step = pl.program_id(0)

    @pl.when(step < n_main)
    def _():
        i = step % nt
        x = x_ref[...]
        ext_ref[0:POOL_HALO, :] = jnp.where(i > 0, halo_ref[...], 0.0)
        ext_ref[POOL_HALO:, :] = x
        gw = x.shape[-1] // len(POOL_WINDOWS)
        pos = lax.broadcasted_iota(jnp.int32, (tm, 1), 0) + i * tm
        ys = []
        for g, w in enumerate(POOL_WINDOWS):
            lo, hi = g * gw, (g + 1) * gw
            xs = x[:, lo:hi]
            s = xs
            for k in range(1, w):
                s = s + ext_ref[POOL_HALO - k:POOL_HALO - k + tm, lo:hi]
            cnt = jnp.minimum(w, pos + 1).astype(F32)
            d = s / cnt - xs
            ys.append(_dot(d.astype(BF16), w_ref[g]))
        y = jnp.concatenate(ys, axis=-1) * scale_ref[...]
        o_ref[...] = _layer_norm(alpha * x + y, g_ref[...], b_ref[...])

    @pl.when(step >= n_main)
    def _():
        o_ref[...] = tail_ref[...]


def _pool_prompt(x, B, T, tail, w_bf, scale, g, b, alpha):
    D = x.shape[1]
    n_tail = tail.shape[0]
    tm = _pick_tile(math.gcd(T, n_tail), 512)
    nt = T // tm
    n_main = B * nt
    hb = tm // POOL_HALO
    G, gw = w_bf.shape[0], w_bf.shape[1]
    main = lambda s: jnp.minimum(s, n_main - 1)
    const2 = lambda s: (0, 0)
    return pl.pallas_call(
        functools.partial(_pool_prompt_kernel, tm=tm, nt=nt, n_main=n_main, alpha=alpha),
        out_shape=jax.ShapeDtypeStruct((B * T + n_tail, D), F32),
        grid=(n_main + n_tail // tm,),
        in_specs=[
            pl.BlockSpec((tm, D), lambda s: (main(s), 0)),
            pl.BlockSpec((POOL_HALO, D), lambda s: (jnp.maximum(main(s) * hb - 1, 0), 0)),
            pl.BlockSpec((tm, D), lambda s: (jnp.maximum(s - n_main, 0), 0)),
            pl.BlockSpec((G, gw, gw), lambda s: (0, 0, 0)),
            pl.BlockSpec((1, D), const2),
            pl.BlockSpec((1, D), const2),
            pl.BlockSpec((1, D), const2),
        ],
        out_specs=pl.BlockSpec((tm, D), lambda s: (s, 0)),
        scratch_shapes=[pltpu.VMEM((tm + POOL_HALO, D), F32)],
        compiler_params=_cparams(("arbitrary",)),
        name="pool_prompt",
    )(x, x, tail, w_bf, scale, g, b)


def _pool_sample_kernel(buf_ref, x_ref, w_ref, scale_ref, g_ref, b_ref, o_ref, *, alpha, pos0):
    nb, ts = buf_ref.shape[0], x_ref.shape[0]
    rows = [buf_ref[r] for r in range(nb)] + [x_ref[t] for t in range(ts)]
    gw = rows[0].shape[-1] // len(POOL_WINDOWS)
    for t in range(ts):
        x = rows[nb + t]
        ys = []
        for g, w in enumerate(POOL_WINDOWS):
            lo, hi = g * gw, (g + 1) * gw
            xs = x[:, lo:hi]
            s = xs
            for k in range(1, w):
                s = s + rows[nb + t - k][:, lo:hi]
            d = s / float(min(w, pos0 + t + 1)) - xs
            ys.append(_dot(d.astype(BF16), w_ref[g]))
        y = jnp.concatenate(ys, axis=-1) * scale_ref[...]
        o_ref[t] = _layer_norm(alpha * x + y, g_ref[...], b_ref[...])


def _pool_sample(buf_t, x_t, w_bf, scale, g, b, alpha, pos0):
    nb, Bs, D = buf_t.shape
    ts = x_t.shape[0]
    bs = _pick_tile(Bs, 32)
    G, gw = w_bf.shape[0], w_bf.shape[1]
    return pl.pallas_call(
        functools.partial(_pool_sample_kernel, alpha=alpha, pos0=pos0),
        out_shape=jax.ShapeDtypeStruct((ts, Bs, D), F32),
        grid=(Bs // bs,),
        in_specs=[
            pl.BlockSpec((nb, bs, D), lambda i: (0, i, 0)),
            pl.BlockSpec((ts, bs, D), lambda i: (0, i, 0)),
            pl.BlockSpec((G, gw, gw), lambda i: (0, 0, 0)),
            pl.BlockSpec((1, D), lambda i: (0, 0)),
            pl.BlockSpec((1, D), lambda i: (0, 0)),
            pl.BlockSpec((1, D), lambda i: (0, 0)),
        ],
        out_specs=pl.BlockSpec((ts, bs, D), lambda i: (0, i, 0)),
        compiler_params=_cparams(("parallel",)),
        name="pool_sample",
    )(buf_t, x_t, w_bf, scale, g, b)


def _ln_ple(x, ff, g_ref, b_ref, p, proj_ref, gw_ref, alpha):
    x2 = _layer_norm(alpha * x + ff, g_ref[...], b_ref[...])
    pp = _dot(p.astype(BF16), proj_ref[...])
    gt = _dot(x2.astype(BF16), gw_ref[...])
    return x2 + pp * _sigmoid(gt)


def _swiglu_partial(xb, wg, wu, wd):
    h = _dot(xb, wg)
    u = _dot(xb, wu)
    return _dot((h * _sigmoid(h) * u).astype(BF16), wd)


def _ffn_kernel(x_ref, wg_ref, wu_ref, wd_ref, g_ref, b_ref, pp_ref, ps_ref, proj_ref, gw_ref, o_ref,
                acc_ref, xb_ref, *, n_main, alpha):
    j = pl.program_id(1)

    @pl.when(j == 0)
    def _():
        acc_ref[...] = jnp.zeros_like(acc_ref)
        xb_ref[...] = x_ref[...].astype(BF16)

    acc_ref[...] += _swiglu_partial(xb_ref[...], wg_ref[...], wu_ref[...], wd_ref[...])

    @pl.when(j == pl.num_programs(1) - 1)
    def _():
        p = jnp.where(pl.program_id(0) < n_main, pp_ref[...], ps_ref[...])
        o_ref[...] = _ln_ple(x_ref[...], acc_ref[...], g_ref, b_ref, p, proj_ref, gw_ref, alpha)


def _ffn_tile(F, target):
    best = LANES
    for t in range(LANES, min(F, target) + 1, LANES):
        if F % t == 0:
            best = t
    return best


def _ple_specs(p_p, p_s, tm):
    n_main = p_p.shape[0] // tm
    head = lambda i, *_: (jnp.minimum(i, n_main - 1), 0)
    tail = lambda i, *_: (jnp.maximum(i - n_main, 0), 0)
    return n_main, [pl.BlockSpec((tm, p_p.shape[1]), head), pl.BlockSpec((tm, p_s.shape[1]), tail)]


def _ffn(x, w_gu, w_down, g, b, p_p, p_s, proj, gate_w, alpha):
    N, D = x.shape
    F = w_down.shape[0]
    P = p_p.shape[1]
    tm = _pick_tile(math.gcd(p_p.shape[0], p_s.shape[0]), 512)
    tf = _ffn_tile(F, 1536)
    nf = F // tf
    n_main, p_specs = _ple_specs(p_p, p_s, tm)
    return pl.pallas_call(
        functools.partial(_ffn_kernel, n_main=n_main, alpha=alpha),
        out_shape=jax.ShapeDtypeStruct((N, D), F32),
        grid=(N // tm, nf),
        in_specs=[
            pl.BlockSpec((tm, D), lambda i, j: (i, 0)),
            pl.BlockSpec((D, tf), lambda i, j: (0, j)),
            pl.BlockSpec((D, tf), lambda i, j: (0, nf + j)),
            pl.BlockSpec((tf, D), lambda i, j: (j, 0)),
            pl.BlockSpec((1, D), lambda i, j: (0, 0)),
            pl.BlockSpec((1, D), lambda i, j: (0, 0)),
            *p_specs,
            pl.BlockSpec((P, D), lambda i, j: (0, 0)),
            pl.BlockSpec((D, D), lambda i, j: (0, 0)),
        ],
        out_specs=pl.BlockSpec((tm, D), lambda i, j: (i, 0)),
        scratch_shapes=[pltpu.VMEM((tm, D), F32), pltpu.VMEM((tm, D), BF16)],
        compiler_params=_cparams(("parallel", "arbitrary")),
        name="ffn_dense",
    )(x, w_gu, w_gu, w_down, g, b, p_p, p_s, proj, gate_w)


def _router_kernel(x_ref, wr_ref, o_ref, rank_ref, run_ref, *, n_e):
    @pl.when(pl.program_id(0) == 0)
    def _():
        run_ref[...] = jnp.zeros_like(run_ref)

    lg = jnp.dot(x_ref[...], wr_ref[...], preferred_element_type=F32, precision=lax.Precision.HIGHEST)
    lane = lax.broadcasted_iota(jnp.int32, lg.shape, 1)
    lg = jnp.where(lane < n_e, lg, -jnp.inf)
    v1 = jnp.max(lg, axis=-1, keepdims=True)
    i1 = jnp.min(jnp.where(lg == v1, lane, LANES), axis=-1, keepdims=True)
    lg2 = jnp.where(lane == i1, -jnp.inf, lg)
    v2 = jnp.max(lg2, axis=-1, keepdims=True)
    i2 = jnp.min(jnp.where(lg2 == v2, lane, LANES), axis=-1, keepdims=True)
    e2 = jnp.exp(v2 - v1)
    den = 1.0 + e2
    gates = jnp.where(lane == i1, 1.0 / den, 0.0) + jnp.where(lane == i2, e2 / den, 0.0)
    o_ref[...] = gates
    tm = gates.shape[0]
    tri = lax.broadcasted_iota(jnp.int32, (tm, tm), 0) >= lax.broadcasted_iota(jnp.int32, (tm, tm), 1)
    sel = jnp.where(gates > 0.0, 1.0, 0.0).astype(BF16)
    incl = run_ref[...] + _dot(jnp.where(tri, 1.0, 0.0).astype(BF16), sel)
    rank_ref[...] = incl
    run_ref[...] = incl[tm - 1:tm, :]


def _router(x, wr_pad, n_e):
    N, D = x.shape
    tm = _pick_tile(N, 512)
    return pl.pallas_call(
        functools.partial(_router_kernel, n_e=n_e),
        out_shape=(jax.ShapeDtypeStruct((N, LANES), F32), jax.ShapeDtypeStruct((N, LANES), F32)),
        grid=(N // tm,),
        in_specs=[pl.BlockSpec((tm, D), lambda i: (i, 0)),
                  pl.BlockSpec((D, LANES), lambda i: (0, 0))],
        out_specs=(pl.BlockSpec((tm, LANES), lambda i: (i, 0)), pl.BlockSpec((tm, LANES), lambda i: (i, 0))),
        scratch_shapes=[pltpu.VMEM((1, LANES), F32)],
        compiler_params=_cparams(("arbitrary",)),
        name="router",
    )(x, wr_pad)


MOE_ROW_TILE = 1024
MOE_SUB_TILE = 512


def _route_plan(gates, incl, n_e, tm):
    N = gates.shape[0]
    n_tiles = -(-TOP_K * N // tm) + n_e
    sel = gates[:, :n_e] > 0.0
    incl = incl[:, :n_e].astype(jnp.int32)
    cnt = incl[-1]
    padded = (cnt + tm - 1) // tm * tm
    ends = jnp.cumsum(padded)
    pos = (ends - padded)[None, :] + incl - 1
    rows = n_tiles * tm
    pos_lo = jnp.minimum(jnp.min(jnp.where(sel, pos, rows), axis=1), rows - 1)
    pos_hi = jnp.max(jnp.where(sel, pos, -1), axis=1)
    pos_hi = jnp.where(pos_hi < 0, pos_lo, pos_hi)
    tile_start = jnp.arange(n_tiles, dtype=jnp.int32) * tm
    tile_e = jnp.minimum(jnp.sum(tile_start[:, None] >= ends[None, :], axis=1), n_e - 1).astype(jnp.int32)
    used = jnp.clip(jnp.take(ends - padded + cnt, tile_e) - tile_start, 0, tm)
    used = jnp.where(tile_start < ends[-1], used, 0).astype(jnp.int32)
    last_e = jnp.max(jnp.where(used > 0, tile_e, 0))
    tile_e = jnp.where(tile_start < ends[-1], tile_e, last_e)
    tails = jnp.concatenate([ends - padded + cnt, ends[-1:] // tm]).astype(jnp.int32)
    return pos_lo.astype(jnp.int32), pos_hi.astype(jnp.int32), tile_e, used, tails, rows


def _row_copy(src_ref, src_row, dst_ref, dst_row, sem):
    return pltpu.make_async_copy(src_ref.at[pl.ds(src_row, 1)], dst_ref.at[pl.ds(dst_row, 1)], sem)


def _dispatch_kernel(lo_ref, hi_ref, tails_ref, x_ref, xs_ref, zero_ref, sem, *, tm, n_e):
    base = pl.program_id(0) * tm

    @pl.when(pl.program_id(0) == 0)
    def _():
        zero_ref[...] = jnp.zeros_like(zero_ref)
        fills = []
        for e in range(n_e):
            first = pl.multiple_of(tails_ref[e] // SUBLANES * SUBLANES, SUBLANES)
            fills.append(pltpu.make_async_copy(zero_ref, xs_ref.at[pl.ds(first, zero_ref.shape[0])], sem))
        for f in fills:
            f.start()
        for f in fills:
            f.wait()

        n_fill = zero_ref.shape[0]
        row_tile = n_fill - SUBLANES
        n_tiles = (xs_ref.shape[0] - n_fill) // row_tile

        def zero_tile(t, c):
            start = pl.multiple_of(t * row_tile, row_tile)
            blank = pltpu.make_async_copy(zero_ref.at[pl.ds(0, row_tile)], xs_ref.at[pl.ds(start, row_tile)], sem)
            blank.start()
            blank.wait()
            return c

        lax.fori_loop(tails_ref[n_e], n_tiles, zero_tile, 0)
        spare = pltpu.make_async_copy(zero_ref, xs_ref.at[pl.ds(n_tiles * row_tile, n_fill)], sem)
        spare.start()
        spare.wait()

    def start(r, c):
        _row_copy(x_ref, r, xs_ref, lo_ref[base + r], sem).start()
        _row_copy(x_ref, r, xs_ref, hi_ref[base + r], sem).start()
        return c

    lax.fori_loop(0, tm, start, 0)
    for _ in range(TOP_K):
        pltpu.make_async_copy(x_ref, xs_ref.at[pl.ds(0, tm)], sem).wait()


def _dispatch(pos_lo, pos_hi, tails, x, rows, row_tile):
    N, D = x.shape
    tm = _pick_tile(N, 256)
    fill = row_tile + SUBLANES
    return pl.pallas_call(
        functools.partial(_dispatch_kernel, tm=tm, n_e=tails.shape[0] - 1),
        out_shape=jax.ShapeDtypeStruct((rows + fill, D), F32),
        grid_spec=pltpu.PrefetchScalarGridSpec(
            num_scalar_prefetch=3,
            grid=(N // tm,),
            in_specs=[pl.BlockSpec((tm, D), lambda i, lo, hi, tl: (i, 0))],
            out_specs=pl.BlockSpec(memory_space=pl.ANY),
            scratch_shapes=[pltpu.VMEM((fill, D), F32), pltpu.SemaphoreType.DMA],
        ),
        compiler_params=_cparams(("arbitrary",)),
        name="moe_dispatch",
    )(pos_lo, pos_hi, tails, x)


def _experts_kernel(te_ref, used_ref, x_ref, wg_ref, wu_ref, wd_ref, o_ref, acc_ref, xb_ref, *, sub):
    r, j = pl.program_id(0), pl.program_id(1)
    used = used_ref[r]
    tm = x_ref.shape[0]
    n_sub = tm // sub

    @pl.when(j == 0)
    def _():
        acc_ref[...] = jnp.zeros_like(acc_ref)

    @pl.when((j == 0) & (used > 0))
    def _():
        xb_ref[...] = x_ref[...].astype(BF16)

    @pl.when(used > tm - sub)
    def _():
        acc_ref[...] += _swiglu_partial(xb_ref[...], wg_ref[...], wu_ref[...], wd_ref[...])

    for s in range(n_sub - 1):
        @pl.when((used > s * sub) & (used <= tm - sub))
        def _():
            rs = pl.ds(s * sub, sub)
            acc_ref[rs, :] += _swiglu_partial(xb_ref[rs, :], wg_ref[...], wu_ref[...], wd_ref[...])

    @pl.when(j == pl.num_programs(1) - 1)
    def _():
        o_ref[...] = acc_ref[...]


def _experts(tile_e, used, xs, w_gu, w_down, layer, tm):
    D = xs.shape[1]
    rows = tile_e.shape[0] * tm
    F = w_down.shape[2]
    tf = _ffn_tile(F, 512)
    nf = F // tf

    def jmap(r, j, used_ref):
        return jnp.where(used_ref[r] > 0, j, nf - 1)

    return pl.pallas_call(
        functools.partial(_experts_kernel, sub=min(MOE_SUB_TILE, tm)),
        out_shape=jax.ShapeDtypeStruct((rows, D), F32),
        grid_spec=pltpu.PrefetchScalarGridSpec(
            num_scalar_prefetch=2,
            grid=(rows // tm, nf),
            in_specs=[
                pl.BlockSpec((tm, D), lambda r, j, te, us: (r, 0)),
                pl.BlockSpec((None, None, D, tf), lambda r, j, te, us: (layer, te[r], 0, jmap(r, j, us))),
                pl.BlockSpec((None, None, D, tf), lambda r, j, te, us: (layer, te[r], 0, nf + jmap(r, j, us))),
                pl.BlockSpec((None, None, tf, D), lambda r, j, te, us: (layer, te[r], jmap(r, j, us), 0)),
            ],
            out_specs=pl.BlockSpec((tm, D), lambda r, j, te, us: (r, 0)),
            scratch_shapes=[pltpu.VMEM((tm, D), F32), pltpu.VMEM((tm, D), BF16)],
        ),
        compiler_params=_cparams(("parallel", "arbitrary")),
        name="moe_experts",
    )(tile_e, used, xs, w_gu, w_gu, w_down)


def _combine_kernel(lo_ref, hi_ref, x_ref, gates_ref, ys_ref, g_ref, b_ref, pp_ref, ps_ref, proj_ref, gw_ref,
                    *refs, tm, n_main, alpha):
    *o_refs, ya_ref, yb_ref, sems = refs
    step = pl.program_id(0)
    slot = step % 2

    last = pl.num_programs(0) - 1

    def start(tile, dst_slot, r):
        _row_copy(ys_ref, lo_ref[tile * tm + r], ya_ref.at[dst_slot], r, sems.at[dst_slot]).start()
        _row_copy(ys_ref, hi_ref[tile * tm + r], yb_ref.at[dst_slot], r, sems.at[dst_slot]).start()

    def wait(w_slot):
        pltpu.make_async_copy(ys_ref.at[pl.ds(0, tm)], ya_ref.at[w_slot], sems.at[w_slot]).wait()
        pltpu.make_async_copy(ys_ref.at[pl.ds(0, tm)], yb_ref.at[w_slot], sems.at[w_slot]).wait()

    @pl.when(step == 0)
    def _():
        lax.fori_loop(0, tm, lambda r, c: (start(step, slot, r), c)[1], 0)

    nxt = jnp.minimum(step + 1, last)
    for r in range(tm):
        start(nxt, 1 - slot, r)

    gates = gates_ref[...]
    lane = lax.broadcasted_iota(jnp.int32, gates.shape, 1)
    sel = gates > 0.0
    e_lo = jnp.min(jnp.where(sel, lane, LANES), axis=-1, keepdims=True)
    e_hi = jnp.max(jnp.where(sel, lane, -1), axis=-1, keepdims=True)
    g_lo = jnp.sum(jnp.where(lane == e_lo, gates, 0.0), axis=-1, keepdims=True)
    g_hi = jnp.where(e_hi != e_lo, jnp.sum(jnp.where(lane == e_hi, gates, 0.0), axis=-1, keepdims=True), 0.0)
    wait(slot)
    ff = g_lo * ya_ref[slot] + g_hi * yb_ref[slot]
    p = jnp.where(step < n_main, pp_ref[...], ps_ref[...])
    out = _ln_ple(x_ref[...], ff, g_ref, b_ref, p, proj_ref, gw_ref, alpha)
    if len(o_refs) == 1:
        o_refs[0][...] = out
    else:
        @pl.when(step < n_main)
        def _():
            o_refs[0][...] = out

        @pl.when(step >= n_main)
        def _():
            o_refs[1][...] = out

    @pl.when(step == last)
    def _():
        wait(1 - slot)


def _combine(pos_lo, pos_hi, x, gates, ys, g, b, p_p, p_s, proj, gate_w, alpha, split):
    N, D = x.shape
    P = p_p.shape[1]
    tm = _pick_tile(math.gcd(p_p.shape[0], p_s.shape[0]), 256)
    row = lambda i, lo, hi: (i, 0)
    const = lambda i, lo, hi: (0, 0)
    n_main, p_specs = _ple_specs(p_p, p_s, tm)
    if split:
        n_p = n_main * tm
        out_shape = (jax.ShapeDtypeStruct((n_p, D), F32), jax.ShapeDtypeStruct((N - n_p, D), F32))
        out_specs = (pl.BlockSpec((tm, D), lambda i, lo, hi: (jnp.minimum(i, n_main - 1), 0)),
                     pl.BlockSpec((tm, D), lambda i, lo, hi: (jnp.maximum(i - n_main, 0), 0)))
    else:
        out_shape = jax.ShapeDtypeStruct((N, D), F32)
        out_specs = pl.BlockSpec((tm, D), row)
    return pl.pallas_call(
        functools.partial(_combine_kernel, tm=tm, n_main=n_main, alpha=alpha),
        out_shape=out_shape,
        grid_spec=pltpu.PrefetchScalarGridSpec(
            num_scalar_prefetch=2,
            grid=(N // tm,),
            in_specs=[pl.BlockSpec((tm, D), row),
                      pl.BlockSpec((tm, LANES), row),
                      pl.BlockSpec(memory_space=pl.ANY),
                      pl.BlockSpec((1, D), const),
                      pl.BlockSpec((1, D), const),
                      *p_specs,
                      pl.BlockSpec((P, D), const),
                      pl.BlockSpec((D, D), const)],
            out_specs=out_specs,
            scratch_shapes=[pltpu.VMEM((2, tm, D), F32), pltpu.VMEM((2, tm, D), F32),
                            pltpu.SemaphoreType.DMA((2,))],
        ),
        compiler_params=_cparams(("arbitrary",)),
        name="moe_combine",
    )(pos_lo, pos_hi, x, gates, ys, g, b, p_p, p_s, proj, gate_w)


def _moe(x, wr_pad, n_e, w_gu, w_down, layer, g, b, p_p, p_s, proj, gate_w, alpha, split=False):
    gates, incl = _router(x, wr_pad, n_e)
    tm = MOE_ROW_TILE
    pos_lo, pos_hi, tile_e, used, tails, rows = _route_plan(gates, incl, n_e, tm)
    xs = _dispatch(pos_lo, pos_hi, tails, x, rows, tm)
    ys = _experts(tile_e, used, xs, w_gu, w_down, layer, tm)
    return _combine(pos_lo, pos_hi, x, gates, ys, g, b, p_p, p_s, proj, gate_w, alpha, split)


def _kv_kernel(x_ref, g_ref, b_ref, w_ref, wvt_ref, wq_ref, kp_ref, vp_ref, ks_ref, vs_ref, kb_ref, vb_ref, vt_ref,
               q_ref, *, n_main, q_scale):
    x = x_ref[...]
    q_ref[...] = (_dot(x.astype(BF16), wq_ref[...]) * q_scale).astype(q_ref.dtype)
    xn = _layer_norm(x, g_ref[...], b_ref[...]).astype(BF16)
    qk = kp_ref.shape[-1]
    k = _dot(xn, w_ref[:, :qk])
    v = _dot(xn, w_ref[:, qk:])

    @pl.when(pl.program_id(0) < n_main)
    def _():
        kp_ref[...] = k
        vp_ref[...] = v

    @pl.when(pl.program_id(0) >= n_main)
    def _():
        ks_ref[...] = k
        vs_ref[...] = v

    kb_ref[...] = k.astype(BF16)
    vb_ref[...] = v.astype(BF16)
    vt_ref[...] = _dot_nt(wvt_ref[...], xn).astype(BF16)


def _shared_kv(x, n_p, g, b, w_kv_bf, w_vt_bf, w_q_bf, q_scale, qk_width):
    N, D = x.shape
    n_s = N - n_p
    W = w_kv_bf.shape[1]
    vw = W - qk_width
    tm = _pick_tile(math.gcd(n_p, n_s), 512)
    n_main = n_p // tm
    row = lambda i: (i, 0)
    const = lambda i: (0, 0)
    head = lambda i: (jnp.minimum(i, n_main - 1), 0)
    tail = lambda i: (jnp.maximum(i - n_main, 0), 0)
    return pl.pallas_call(
        functools.partial(_kv_kernel, n_main=n_main, q_scale=q_scale),
        out_shape=(jax.ShapeDtypeStruct((n_p, qk_width), F32), jax.ShapeDtypeStruct((n_p, vw), F32),
                   jax.ShapeDtypeStruct((n_s, qk_width), F32), jax.ShapeDtypeStruct((n_s, vw), F32),
                   jax.ShapeDtypeStruct((N, qk_width), BF16), jax.ShapeDtypeStruct((N, vw), BF16),
                   jax.ShapeDtypeStruct((vw, N), BF16), jax.ShapeDtypeStruct((N, w_q_bf.shape[1]), BF16)),
        grid=(N // tm,),
        in_specs=[pl.BlockSpec((tm, D), row), pl.BlockSpec((1, D), const), pl.BlockSpec((1, D), const),
                  pl.BlockSpec((D, W), const), pl.BlockSpec((vw, D), const),
                  pl.BlockSpec(w_q_bf.shape, const)],
        out_specs=(pl.BlockSpec((tm, qk_width), head), pl.BlockSpec((tm, vw), head),
                   pl.BlockSpec((tm, qk_width), tail), pl.BlockSpec((tm, vw), tail),
                   pl.BlockSpec((tm, qk_width), row), pl.BlockSpec((tm, vw), row),
                   pl.BlockSpec((vw, tm), lambda i: (0, i)), pl.BlockSpec((tm, w_q_bf.shape[1]), row)),
        compiler_params=_cparams(("arbitrary",)),
        name="shared_kv",
    )(x, g, b, w_kv_bf, w_vt_bf, w_q_bf)


def _proj_kernel(x_ref, w_ref, o_ref, *, scale):
    o_ref[...] = (_dot(x_ref[...].astype(BF16), w_ref[...]) * scale).astype(o_ref.dtype)


def _proj(x, w_bf, scale):
    N, D = x.shape
    W = w_bf.shape[1]
    tm = _pick_tile(N, 512)
    return pl.pallas_call(
        functools.partial(_proj_kernel, scale=scale),
        out_shape=jax.ShapeDtypeStruct((N, W), BF16),
        grid=(N // tm,),
        in_specs=[pl.BlockSpec((tm, D), lambda i: (i, 0)),
                  pl.BlockSpec((D, W), lambda i: (0, 0))],
        out_specs=pl.BlockSpec((tm, W), lambda i: (i, 0)),
        compiler_params=_cparams(("parallel",)),
        name="q_proj",
    )(x, w_bf)


def _out_ln_kernel(op_ref, os_ref, w_ref, x_ref, g_ref, b_ref, y_ref, *, n_main, alpha):
    o = jnp.where(pl.program_id(0) < n_main, op_ref[...], os_ref[...])
    y_ref[...] = _layer_norm(alpha * x_ref[...] + _dot(o, w_ref[...]), g_ref[...], b_ref[...])


def _out_ln(o_p, o_s, w_bf, x, g, b, alpha):
    N, D = x.shape
    W = o_p.shape[1]
    tm = _pick_tile(math.gcd(o_p.shape[0], o_s.shape[0]), 512)
    n_main = o_p.shape[0] // tm
    return pl.pallas_call(
        functools.partial(_out_ln_kernel, n_main=n_main, alpha=alpha),
        out_shape=jax.ShapeDtypeStruct((N, D), F32),
        grid=(N // tm,),
        in_specs=[pl.BlockSpec((tm, W), lambda i: (jnp.minimum(i, n_main - 1), 0)),
                  pl.BlockSpec((tm, W), lambda i: (jnp.maximum(i - n_main, 0), 0)),
                  pl.BlockSpec((W, D), lambda i: (0, 0)),
                  pl.BlockSpec((tm, D), lambda i: (i, 0)),
                  pl.BlockSpec((1, D), lambda i: (0, 0)),
                  pl.BlockSpec((1, D), lambda i: (0, 0))],
        out_specs=pl.BlockSpec((tm, D), lambda i: (i, 0)),
        compiler_params=_cparams(("parallel",)),
        name="attn_out_ln",
    )(o_p, o_s, w_bf, x, g, b)


def _lambda_value(lam_ref, lam_init):
    lq = lam_ref[...]
    s1 = jnp.sum(lq[0:1] * lq[1:2], axis=-1, keepdims=True)
    s2 = jnp.sum(lq[2:3] * lq[3:4], axis=-1, keepdims=True)
    return jnp.exp(s1) - jnp.exp(s2) + lam_init


def _head_norm(o, g, lam_init):
    return o * lax.rsqrt(jnp.mean(o * o, axis=-1, keepdims=True) + RMS_EPS) * g * (1.0 - lam_init)


def _softmax_update(s, v, m_ref, l_ref, acc_ref, idx):
    m_old = m_ref[idx]
    m_new = jnp.maximum(m_old, jnp.max(s, axis=-1, keepdims=True))
    a = jnp.exp(m_old - m_new)
    p = jnp.exp(s - m_new)
    l_ref[idx] = a * l_ref[idx] + jnp.sum(p, axis=-1, keepdims=True)
    acc_ref[idx] = a * acc_ref[idx] + _dot(p.astype(BF16), v)
    m_ref[idx] = m_new


KPOS_RADIX = 256
PROMPT_HEADS_PER_STEP = 8
STAT_ROWS = 8
SUM_ROWS = 16


def _attn_prompt_kernel(ti_ref, tj_ref, slopes_ref, lam_ref, q_ref, k_ref, vt_ref, g_ref, o_ref,
                        qa_ref, m_ref, acc_ref, *, tq, hd, hp, lam_init):
    hg, pair = pl.program_id(1), pl.program_id(2)
    i, j = ti_ref[pair], tj_ref[pair]
    vd = 2 * hd

    @pl.when(j == 0)
    def _():
        m_ref[...] = jnp.full_like(m_ref, NEG_INF)
        acc_ref[...] = jnp.zeros_like(acc_ref)
        lane = lax.broadcasted_iota(jnp.int32, (tq, vd), 1)
        for hh in range(hp):
            q = q_ref[:, hh * vd:(hh + 1) * vd]
            slope = slopes_ref[hg * hp + hh]
            aug = jnp.where(lane == 0, slope * KPOS_RADIX, jnp.where(lane == 1, slope, 0.0)).astype(BF16)
            qa_ref[hh, 0:tq, 0:vd] = jnp.where(lane < hd, q, jnp.zeros_like(q))
            qa_ref[hh, tq:, 0:vd] = jnp.where(lane >= hd, q, jnp.zeros_like(q))
            qa_ref[hh, 0:tq, vd:] = aug
            qa_ref[hh, tq:, vd:] = aug

    def block(diagonal):
        kpos = lax.broadcasted_iota(jnp.int32, (tq, vd), 0) + j * tq
        lane = lax.broadcasted_iota(jnp.int32, (tq, vd), 1)
        kaug = jnp.where(lane == 0, kpos // KPOS_RADIX, jnp.where(lane == 1, kpos % KPOS_RADIX, 0))
        kaug = kaug.astype(F32).astype(BF16)
        ones = jnp.ones((SUM_ROWS, tq), BF16)
        if diagonal:
            kr = lax.broadcasted_iota(jnp.int32, (tq, 2 * tq), 0)
            qc = lax.broadcasted_iota(jnp.int32, (tq, 2 * tq), 1)
            visible = jnp.where(qc >= tq, qc - tq, qc) >= kr

        def scores(hh):
            ka = jnp.concatenate([k_ref[:, hh * vd:(hh + 1) * vd], kaug], axis=1)
            return _dot_nt(ka, qa_ref[hh])

        s_next = scores(0)
        for hh in range(hp):
            s = s_next
            if hh + 1 < hp:
                s_next = scores(hh + 1)
            if diagonal:
                s = jnp.where(visible, s, NEG_INF)
            m_old = m_ref[hh]
            m_new = jnp.maximum(m_old, jnp.max(s, axis=0, keepdims=True))
            a = jnp.exp(m_old[0:1] - m_new[0:1])
            p = jnp.exp(s - m_new[0:1]).astype(BF16)
            vt1 = jnp.concatenate([vt_ref[hh * vd:(hh + 1) * vd, :], ones], axis=0)
            acc_ref[hh] = a * acc_ref[hh] + _dot(vt1, p)
            m_ref[hh] = m_new

    @pl.when(j < i)
    def _():
        block(False)

    @pl.when(j == i)
    def _():
        block(True)
        lam = _lambda_value(lam_ref, lam_init)
        for hh in range(hp):
            acc = acc_ref[hh]
            ot = acc[:vd] / acc[vd:vd + 1]
            ot = ot[:, :tq] - lam * ot[:, tq:]
            ot = ot * lax.rsqrt(jnp.mean(ot * ot, axis=0, keepdims=True) + RMS_EPS)
            o = ot.T * g_ref[:, hh * vd:(hh + 1) * vd] * (1.0 - lam_init)
            o_ref[:, hh * vd:(hh + 1) * vd] = o.astype(o_ref.dtype)


def _attn_prompt(q, k, vt, slopes, lam_qk, subln_g, B, T, H, lam_init, hp):
    W = q.shape[1]
    vd = W // H
    hd = vd // 2
    tq = _pick_tile(T, 512)
    nq = T // tq
    assert T <= KPOS_RADIX * KPOS_RADIX and H % hp == 0
    pairs = [(i, j) for i in range(nq) for j in range(i + 1)]
    ti = jnp.asarray([p[0] for p in pairs], jnp.int32)
    tj = jnp.asarray([p[1] for p in pairs], jnp.int32)
    return pl.pallas_call(
        functools.partial(_attn_prompt_kernel, tq=tq, hd=hd, hp=hp, lam_init=lam_init),
        out_shape=jax.ShapeDtypeStruct((B * T, W), BF16),
        grid_spec=pltpu.PrefetchScalarGridSpec(
            num_scalar_prefetch=2,
            grid=(B, H // hp, len(pairs)),
            in_specs=[
                pl.BlockSpec(memory_space=pltpu.SMEM),
                pl.BlockSpec(lam_qk.shape, lambda b, h, p, ti, tj: (0, 0)),
                pl.BlockSpec((tq, hp * vd), lambda b, h, p, ti, tj: (b * nq + ti[p], h)),
                pl.BlockSpec((tq, hp * vd), lambda b, h, p, ti, tj: (b * nq + tj[p], h)),
                pl.BlockSpec((hp * vd, tq), lambda b, h, p, ti, tj: (h, b * nq + tj[p])),
                pl.BlockSpec((1, hp * vd), lambda b, h, p, ti, tj: (0, h)),
            ],
            out_specs=pl.BlockSpec((tq, hp * vd), lambda b, h, p, ti, tj: (b * nq + ti[p], h)),
            scratch_shapes=[pltpu.VMEM((hp, 2 * tq, 2 * vd), BF16), pltpu.VMEM((hp, STAT_ROWS, 2 * tq), F32),
                            pltpu.VMEM((hp, vd + SUM_ROWS, 2 * tq), F32)],
        ),
        compiler_params=_cparams(("parallel", "parallel", "arbitrary")),
        name="attn_prompt",
    )(ti, tj, slopes, lam_qk, q, k, vt, subln_g)


SAMPLE_PAGES_PER_STEP = 8


def _attn_sample_kernel(pt_ref, lam_ref, qz_ref, *refs, gp, n_steps, lam_init):
    k_refs, v_refs = refs[:gp], refs[gp:2 * gp]
    kn_ref, vn_ref, bias_ref, biasn_ref, g_ref, o_ref, m_ref, l_ref, acc_ref = refs[2 * gp:]
    st = pl.program_id(1)
    rows = k_refs[0].shape[1]

    @pl.when(st == 0)
    def _():
        m_ref[...] = jnp.full_like(m_ref, NEG_INF)
        l_ref[...] = jnp.zeros_like(l_ref)
        acc_ref[...] = jnp.zeros_like(acc_ref)

    qz = qz_ref[0]
    for step in range(n_steps):
        @pl.when(st == step)
        def _():
            ss = []
            for g in range(gp):
                pg = step * gp + g
                ss.append(_dot_nt(qz, k_refs[g][0].astype(BF16)) + bias_ref[:, pg * rows:(pg + 1) * rows])
            vb = jnp.concatenate([v_refs[g][0].astype(BF16) for g in range(gp)], axis=0)
            _softmax_update(jnp.concatenate(ss, axis=1), vb, m_ref, l_ref, acc_ref, 0)

    @pl.when(st == n_steps - 1)
    def _():
        _softmax_update(_dot_nt(qz, kn_ref[0]) + biasn_ref[...], vn_ref[0], m_ref, l_ref, acc_ref, 0)
        lam = _lambda_value(lam_ref, lam_init)
        o = acc_ref[0] / l_ref[0]
        half = o.shape[0] // 2
        o = o[:half] - lam * o[half:]
        o_ref[0] = _head_norm(o, g_ref[...], lam_init).astype(o_ref.dtype)


def _attn_sample(page_table, qz, cache_k2, cache_v2, k_new, v_new, bias, bias_new, g_rows, lam_qk, lam_init):
    Bs, R, vd = qz.shape
    n_pages = page_table.shape[1]
    gp = math.gcd(n_pages, SAMPLE_PAGES_PER_STEP)
    n_steps = n_pages // gp
    rows = cache_k2.shape[1]
    rn = k_new.shape[1]
    pt = page_table.reshape(-1)

    def page_spec(g):
        return pl.BlockSpec((1, rows, vd), lambda b, s, pt_ref: (pt_ref[b * n_pages + s * gp + g], 0, 0))

    const2 = lambda b, s, pt_ref: (0, 0)
    seq3 = lambda b, s, pt_ref: (b, 0, 0)
    return pl.pallas_call(
        functools.partial(_attn_sample_kernel, gp=gp, n_steps=n_steps, lam_init=lam_init),
        out_shape=jax.ShapeDtypeStruct((Bs, R // 2, vd), BF16),
        grid_spec=pltpu.PrefetchScalarGridSpec(
            num_scalar_prefetch=1,
            grid=(Bs, n_steps),
            in_specs=[pl.BlockSpec(lam_qk.shape, const2), pl.BlockSpec((1, R, vd), seq3)]
            + [page_spec(g) for g in range(gp)] + [page_spec(g) for g in range(gp)]
            + [pl.BlockSpec((1, rn, vd), seq3), pl.BlockSpec((1, rn, vd), seq3),
               pl.BlockSpec(bias.shape, const2), pl.BlockSpec(bias_new.shape, const2),
               pl.BlockSpec(g_rows.shape, const2)],
            out_specs=pl.BlockSpec((1, R // 2, vd), seq3),
            scratch_shapes=[pltpu.VMEM((1, R, 1), F32), pltpu.VMEM((1, R, 1), F32),
                            pltpu.VMEM((1, R, vd), F32)],
        ),
        compiler_params=_cparams(("parallel", "arbitrary")),
        name="attn_sample",
    )(pt, lam_qk, qz, *([cache_k2] * gp), *([cache_v2] * gp), k_new, v_new, bias, bias_new, g_rows)


def _alibi_slopes(H):
    return 2.0 ** (-8.0 * np.arange(1, H + 1, dtype=np.float64) / H)


def _sample_bias_tables(H, Ts, page, n_pages):
    slopes = _alibi_slopes(H)
    pos0 = n_pages * page
    hq = np.tile(np.repeat(np.arange(H), Ts), 2)
    tq = np.tile(np.arange(Ts), 2 * H)
    key = np.repeat(np.arange(n_pages * page), H)
    hk = np.tile(np.arange(H), n_pages * page)
    dist = (pos0 + tq)[:, None] - key[None, :]
    bias = np.where(hq[:, None] == hk[None, :], -slopes[hq][:, None] * dist, NEG_INF)
    tn = np.repeat(np.arange(Ts), H)
    hn = np.tile(np.arange(H), Ts)
    dn = tq[:, None] - tn[None, :]
    bias_new = np.where((hq[:, None] == hn[None, :]) & (dn >= 0), -slopes[hq][:, None] * dn, NEG_INF)
    return jnp.asarray(bias, F32), jnp.asarray(bias_new, F32)


def kernel(x_prompt, x_sample, state_pool, cache_k, cache_v, page_table, p_prompt, p_sample, pool_w, pool_scale, ln1_g, ln1_b, ln2_g, ln2_b, ln_kv_g, ln_kv_b, w_kv, w_q, lam_qk, subln_g, w_o, dense_gu, dense_down, w_router, moe_gu, moe_down, ple_proj, ple_gate):
    B, T, D = x_prompt.shape
    Bs, Ts, _ = x_sample.shape
    depth = ln1_g.shape[0]
    n_a = pool_w.shape[0]
    n_pool, page, H, vd = cache_v.shape
    hd = vd // 2
    qk_width = H * 2 * hd
    n_e = w_router.shape[-1]
    n_pages = page_table.shape[1]
    pos0 = n_pages * page
    n_p, n_s = B * T, Bs * Ts
    n_all = n_p + n_s
    alpha = (2 * depth) ** 0.25
    row = lambda a: a.reshape(1, -1)

    bf = lambda w: w.astype(BF16)
    moe_gu_bf, moe_down_bf = bf(moe_gu), bf(moe_down)
    wr_pad = jnp.pad(w_router, ((0, 0), (0, 0), (0, LANES - n_e)))

    p_p, p_s = p_prompt.reshape(depth, n_p, -1), p_sample.reshape(depth, n_s, -1)
    slopes_np = _alibi_slopes(H)
    assert np.all((slopes_np * KPOS_RADIX).astype(BF16).astype(np.float64) == slopes_np * KPOS_RADIX)
    assert np.all(slopes_np.astype(BF16).astype(np.float64) == slopes_np)
    slopes = jnp.asarray(slopes_np, F32)
    bias, bias_new = _sample_bias_tables(H, Ts, page, n_pages)

    x_rows = x_prompt.reshape(n_p, D)
    x_s = x_sample
    x_all = None
    pool_p, pool_s = [], []
    kv = None
    for i in range(depth):
        if i < n_a:
            pool_p.append(jnp.stack([x_rows[(bb + 1) * T - POOL_BUF:(bb + 1) * T] for bb in range(B)]))
            pool_s.append(jnp.concatenate([state_pool[i][:, Ts:], x_s], axis=1))
            pw = bf(pool_w[i])
            x1_s = _pool_sample(state_pool[i].transpose(1, 0, 2), x_s.transpose(1, 0, 2), pw,
                                row(pool_scale[i]), row(ln1_g[i]), row(ln1_b[i]), alpha, pos0)
            x1 = _pool_prompt(x_rows, B, T, x1_s.transpose(1, 0, 2).reshape(n_s, D), pw, row(pool_scale[i]),
                              row(ln1_g[i]), row(ln1_b[i]), alpha)
        else:
            jj = i - n_a
            lam_init = 0.8 - 0.6 * math.exp(-0.3 * i)
            if kv is None:
                kv = _shared_kv(x_all, n_p, row(ln_kv_g), row(ln_kv_b), bf(w_kv), bf(w_kv[:, qk_width:].T),
                                bf(w_q[jj]), hd ** -0.5, qk_width)
                q = kv[7]
            else:
                q = _proj(x_all, bf(w_q[jj]), hd ** -0.5)
            k_b, v_b, v_t = kv[4:7]
            o = _attn_prompt(q, k_b, v_t, slopes, lam_qk[jj], row(subln_g[jj]), B, T, H, lam_init,
                             math.gcd(H, PROMPT_HEADS_PER_STEP))
            q_s = q[n_p:].reshape(Bs, Ts, H, vd).transpose(0, 2, 1, 3).reshape(Bs, H * Ts, vd)
            lane = jnp.arange(vd)
            qz = jnp.concatenate([jnp.where(lane < hd, q_s, 0), jnp.where(lane >= hd, q_s, 0)], axis=1)
            g_rows = jnp.repeat(subln_g[jj].reshape(H, vd), Ts, axis=0)
            o_s = _attn_sample(page_table, qz, cache_k.reshape(n_pool, page * H, 2 * hd),
                               cache_v.reshape(n_pool, page * H, vd),
                               k_b[n_p:].reshape(Bs, Ts * H, 2 * hd), v_b[n_p:].reshape(Bs, Ts * H, vd),
                               bias, bias_new, g_rows, lam_qk[jj], lam_init)
            o_s = o_s.reshape(Bs, H, Ts, vd).transpose(0, 2, 1, 3).reshape(n_s, H * vd)
            x1 = _out_ln(o, o_s, bf(w_o[jj]), x_all, row(ln1_g[i]), row(ln1_b[i]), alpha)
        tail_args = (row(ln2_g[i]), row(ln2_b[i]), p_p[i], p_s[i], bf(ple_proj[i]), bf(ple_gate[i]), alpha)
        last_moe = i % 2 == 1 and i + 1 == depth
        if i % 2 == 0:
            x_all = _ffn(x1, bf(dense_gu[i // 2]), bf(dense_down[i // 2]), *tail_args)
        elif not last_moe:
            x_all = _moe(x1, wr_pad[i // 2], n_e, moe_gu_bf, moe_down_bf, i // 2, *tail_args)
        else:
            y_p, y_s = _moe(x1, wr_pad[i // 2], n_e, moe_gu_bf, moe_down_bf, i // 2, *tail_args, split=True)
        if not last_moe:
            x_rows = x_all
            x_s = x_all[n_p:].reshape(Bs, Ts, D)
            y_p, y_s = x_all[:n_p], x_all[n_p:]

    k_p, v_p, k_s, v_s = kv[:4]
    return (y_p.reshape(B, T, D), y_s.reshape(Bs, Ts, D), jnp.stack(pool_p), jnp.stack(pool_s),
            k_p.reshape(B, T, H, 2 * hd), v_p.reshape(B, T, H, vd),
            k_s.reshape(Bs, Ts, H, 2 * hd), v_s.reshape(Bs, Ts, H, vd))
```
